```python
import jax
import jax.numpy as jnp
from jax import lax
import numpy as np

D_MODEL = 1024
BATCH = 4
SEQ = 4096
DEPTH = 2

HEAD_DIM = 64
MIX_WIDTH = D_MODEL // 4
N_BRANCH = 4
NORM_EPS = 1e-6
MASK_VALUE = -1e9
EXP_CLIP = 60.0

HG_HEADS = MIX_WIDTH // HEAD_DIM
GLA_HEADS = 4
GLA_DV = MIX_WIDTH // GLA_HEADS
GLA_DK = GLA_DV // 2
GLA_RANK = 16
GLA_TAU = 16.0
CHUNK = 64
RW_HEADS = MIX_WIDTH // HEAD_DIM
RW_DECAY_LORA = 32
RW_AAA_LORA = 32
RW_MV_LORA = 32
RW_GATE_LORA = 64
RW_GN_EPS = 64e-5
DIL_PAIRS = ((128, 1), (512, 4), (2048, 16))
N_DIL = 3
AT_HEADS = MIX_WIDTH // HEAD_DIM
AT_WIDTH = N_DIL * AT_HEADS * HEAD_DIM
Q_BLOCK = 128
ROPE_THETA = 10000.0
D_FF = 11 * D_MODEL // 4
CONV_W = 3

HG_WIDTH = 4 * MIX_WIDTH
GLA_WIDTH = 2 * GLA_HEADS * GLA_DK + 2 * MIX_WIDTH + GLA_RANK
RW_WIDTH = 3 * MIX_WIDTH + RW_DECAY_LORA + RW_AAA_LORA + RW_GATE_LORA
GATE_WIDTH = N_BRANCH * D_MODEL
N_IN = HG_WIDTH + GLA_WIDTH + RW_WIDTH + 3 * AT_WIDTH + GATE_WIDTH

kernel_name = "hybrid_hgrn2_gla_rwkv7_dilated_gated_merge"

F32 = jnp.float32


def _split(z, widths):
    out, off = [], 0
    for w in widths:
        out.append(z[..., off:off + w])
        off += w
    return out


def rmsnorm(x, w):
    xf = x.astype(F32)
    y = xf * lax.rsqrt(jnp.mean(xf * xf, axis=-1, keepdims=True) + NORM_EPS)
    return (y * w).astype(x.dtype)


def head_rmsnorm(o, n_heads, w):
    B, S, W = o.shape
    of = o.astype(F32).reshape(B, S, n_heads, W // n_heads)
    of = of * lax.rsqrt(jnp.mean(of * of, axis=-1, keepdims=True) + NORM_EPS)
    return (of.reshape(B, S, W) * w).astype(o.dtype)


def to_heads(t, n_heads):
    B, S, W = t.shape
    return t.reshape(B, S, n_heads, W // n_heads).transpose(0, 2, 1, 3)


def from_heads(t):
    B, H, S, Dh = t.shape
    return t.transpose(0, 2, 1, 3).reshape(B, S, H * Dh)


def chunk_gated_linear_attn(q, k, v, log_g):
    B, H, S, Dk = q.shape
    Dv = v.shape[-1]
    nc = S // CHUNK

    def to_chunks(t):
        return jnp.moveaxis(t.astype(F32).reshape(B, H, nc, CHUNK, t.shape[-1]), 2, 0)

    xs = (to_chunks(q), to_chunks(k), to_chunks(v), to_chunks(log_g))
    causal = jnp.tril(jnp.ones((CHUNK, CHUNK), dtype=bool))[:, :, None]

    def step(state, inp):
        qb, kb, vb, gb = inp
        b = jnp.cumsum(gb, axis=2)
        diff = b[:, :, :, None, :] - b[:, :, None, :, :]
        decay = jnp.where(causal, jnp.exp(jnp.where(causal, diff, 0.0)), 0.0)
        scores = jnp.einsum('bhtd,bhsd,bhtsd->bhts', qb, kb, decay)
        o = jnp.einsum('bhts,bhsv->bhtv', scores, vb) + jnp.einsum('bhtk,bhkv->bhtv', qb * jnp.exp(b), state)
        b_last = b[:, :, -1:, :]
        k_dec = kb * jnp.exp(b_last - b)
        state = jnp.exp(b_last[:, :, 0, :, None]) * state + jnp.einsum('bhsk,bhsv->bhkv', k_dec, vb)
        return state, o

    s0 = jnp.zeros((B, H, Dk, Dv), F32)
    _, o = lax.scan(step, s0, xs)
    return jnp.moveaxis(o, 0, 2).reshape(B, H, S, Dv).astype(v.dtype)


def hgrn2_mixer(z, lb, norm_w):
    q_raw, f_raw, i_raw, g_raw = _split(z, (MIX_WIDTH,) * 4)
    fz = f_raw.astype(F32)
    log_f = jax.nn.log_sigmoid(fz) + jnp.log1p(lb * jnp.exp(jnp.minimum(-fz, EXP_CLIP)))
    k = (1.0 - lb) * jax.nn.sigmoid(-fz)
    q = jax.nn.silu(q_raw)
    o = chunk_gated_linear_attn(to_heads(q, HG_HEADS), to_heads(k, HG_HEADS),
                                to_heads(i_raw, HG_HEADS), to_heads(log_f, HG_HEADS))
    return head_rmsnorm(from_heads(o), HG_HEADS, norm_w) * jax.nn.sigmoid(g_raw)


def gla_mixer(z, a_w2, a_b, norm_w):
    q, k, v, g, a_low = _split(z, (GLA_HEADS * GLA_DK, GLA_HEADS * GLA_DK, MIX_WIDTH, MIX_WIDTH, GLA_RANK))
    log_alpha = jax.nn.log_sigmoid((a_low @ a_w2 + a_b).astype(F32)) / GLA_TAU
    o = chunk_gated_linear_attn(to_heads(q, GLA_HEADS) * (GLA_DK ** -0.5), to_heads(k, GLA_HEADS),
                                to_heads(v, GLA_HEADS), to_heads(log_alpha, GLA_HEADS))
    return head_rmsnorm(from_heads(o), GLA_HEADS, norm_w) * jax.nn.silu(g)


def rwkv7_scan(r, w, k, v, kk, a):
    B, S, H, Dh = r.shape

    def step(state, inp):
        r_t, w_t, k_t, v_t, kk_t, a_t = inp
        sa = jnp.einsum('bhvk,bhk->bhv', state, -kk_t)
        state = (state * w_t[:, :, None, :] + sa[..., None] * (kk_t * a_t)[:, :, None, :]
                 + v_t[..., None] * k_t[:, :, None, :])
        return state, jnp.einsum('bhvk,bhk->bhv', state, r_t)

    xs = tuple(jnp.moveaxis(t, 1, 0) for t in (r, w, k, v, kk, a))
    s0 = jnp.zeros((B, H, Dh, Dh), F32)
    _, ys = lax.scan(step, s0, xs)
    return jnp.moveaxis(ys, 0, 1)


def rwkv7_mixer(z, mu, w0, w2, a0, a2, g2, k_k, k_a, r_k, ln_w, ln_b, v_first, v0, v1, v2):
    B, S, _ = z.shape
    z_prev = jnp.pad(z, ((0, 0), (1, 0), (0, 0)))[:, :-1]
    zs = z + mu * (z_prev - z)
    r, wl, k, v, al, gl = _split(zs, (MIX_WIDTH, RW_DECAY_LORA, MIX_WIDTH, MIX_WIDTH, RW_AAA_LORA, RW_GATE_LORA))
    w = -jax.nn.softplus(-(w0 + jnp.tanh(wl) @ w2)) - 0.5
    decay = jnp.exp(-jnp.exp(w.astype(F32)))
    a = jax.nn.sigmoid(a0 + al @ a2)
    g = jax.nn.sigmoid(gl) @ g2
    if v_first is not None:
        v = v + (v_first - v) * jax.nn.sigmoid(v0 + (v @ v1) @ v2)

    def hs(t):
        return t.astype(F32).reshape(B, S, RW_HEADS, HEAD_DIM)

    kk = hs(k * k_k)
    kk = kk / jnp.maximum(jnp.sqrt(jnp.sum(kk * kk, axis=-1, keepdims=True)), 1e-12)
    k_mod = hs(k * (1.0 + (a - 1.0) * k_a))
    rh, vh = hs(r), hs(v)
    y = rwkv7_scan(rh, hs(decay), k_mod, vh, kk, hs(a))
    mean = jnp.mean(y, axis=-1, keepdims=True)
    var = jnp.mean(jnp.square(y - mean), axis=-1, keepdims=True)
    yn = ((y - mean) * lax.rsqrt(var + RW_GN_EPS)).reshape(B, S, MIX_WIDTH) * ln_w + ln_b
    bonus = (jnp.sum(rh * k_mod * r_k.reshape(RW_HEADS, HEAD_DIM), axis=-1, keepdims=True) * vh).reshape(B, S, MIX_WIDTH)
    return ((yn + bonus) * g).astype(z.dtype), v


def rope(x, pos):
    half = HEAD_DIM // 2
    inv = ROPE_THETA ** (-jnp.arange(half, dtype=F32) / half)
    ang = pos.astype(F32)[:, None] * inv[None, :]
    cos, sin = jnp.cos(ang), jnp.sin(ang)
    xf = x.astype(F32)
    x1, x2 = xf[..., :half], xf[..., half:]
    return jnp.concatenate([x1 * cos - x2 * sin, x1 * sin + x2 * cos], axis=-1).astype(x.dtype)


def dilated_window_attention(q, k, v):
    B, G, H, S, Dh = q.shape
    nb = S // Q_BLOCK
    scale = Dh ** -0.5

    def block(start):
        t = start + jnp.arange(Q_BLOCK)
        outs, lses = [], []
        for g, (window, dil) in enumerate(DIL_PAIRS):
            n_keys = window // dil + 1
            idx = t[:, None] - dil * jnp.arange(n_keys)[None, :]
            valid = idx >= 0
            idx = jnp.maximum(idx, 0)
            qb = lax.dynamic_slice_in_dim(q[:, g], start, Q_BLOCK, axis=2)
            kg = jnp.take(k[:, g], idx, axis=2)
            vg = jnp.take(v[:, g], idx, axis=2)
            s = jnp.einsum('bhqd,bhqnd->bhqn', qb, kg).astype(F32) * scale
            s = jnp.where(valid, s, MASK_VALUE)
            lse = jax.nn.logsumexp(s, axis=-1)
            p = jnp.exp(s - lse[..., None])
            outs.append(jnp.einsum('bhqn,bhqnd->bhqd', p, vg.astype(F32)))
            lses.append(lse)
        wts = jax.nn.softmax(jnp.stack(lses, axis=0), axis=0)
        return jnp.sum(wts[..., None] * jnp.stack(outs, axis=0), axis=0)

    ob = lax.map(block, jnp.arange(nb) * Q_BLOCK)
    return jnp.moveaxis(ob, 0, 2).reshape(B, H, S, Dh).astype(q.dtype)


def dilated_mixer(z, pos):
    B, S, _ = z.shape
    q, k, v = _split(z, (AT_WIDTH,) * 3)

    def gh(t):
        return t.reshape(B, S, N_DIL, AT_HEADS, HEAD_DIM).transpose(0, 2, 3, 1, 4)

    o = dilated_window_attention(rope(gh(q), pos), rope(gh(k), pos), gh(v))
    return from_heads(o)


def causal_dwconv(a, w, b):
    S = a.shape[1]
    ap = jnp.pad(a, ((0, 0), (CONV_W - 1, 0), (0, 0)))
    out = b + w[0] * a
    for j in range(1, CONV_W):
        out = out + w[j] * ap[:, CONV_W - 1 - j:CONV_W - 1 - j + S]
    return out


def setup_inputs(seed: int = 0) -> dict:
    key = jax.random.key(seed)
    ks = jax.random.split(key, 32)
    n = lambda i, shape, s: jax.random.normal(ks[i], shape, F32) * s
    one = lambda i, shape: 1.0 + 0.02 * jax.random.normal(ks[i], shape, F32)
    L, W = DEPTH, MIX_WIDTH
    return {
        "x": n(0, (BATCH, SEQ, D_MODEL), 1.0),
        "norm_mix_w": one(1, (L, D_MODEL)),
        "norm_ffn_w": one(2, (L, D_MODEL)),
        "norm_final_w": one(3, (D_MODEL,)),
        "w_in": n(4, (L, D_MODEL, N_IN), D_MODEL ** -0.5),
        "hg_lb_table": n(5, (L, W), 0.5),
        "hg_norm_w": one(6, (L, W)),
        "gla_a_w2": n(7, (L, GLA_RANK, GLA_HEADS * GLA_DK), GLA_RANK ** -0.5),
        "gla_a_b": n(8, (L, GLA_HEADS * GLA_DK), 0.1),
        "gla_norm_w": one(9, (L, W)),
        "rw_mu": jax.random.uniform(ks[10], (L, RW_WIDTH), F32),
        "rw_w0": jax.random.uniform(ks[11], (L, W), F32, minval=-5.0, maxval=0.0),
        "rw_w2": n(12, (L, RW_DECAY_LORA, W), 0.5 * RW_DECAY_LORA ** -0.5),
        "rw_a0": n(13, (L, W), 0.1),
        "rw_a2": n(14, (L, RW_AAA_LORA, W), RW_AAA_LORA ** -0.5),
        "rw_g2": n(15, (L, RW_GATE_LORA, W), RW_GATE_LORA ** -0.5),
        "rw_k_k": 0.85 + n(16, (L, W), 0.1),
        "rw_k_a": 1.0 + n(17, (L, W), 0.1),
        "rw_r_k": n(18, (L, W), 0.1),
        "rw_ln_w": one(19, (L, W)),
        "rw_ln_b": n(20, (L, W), 0.02),
        "rw_v0": n(21, (L - 1, W), 0.1),
        "rw_v1": n(22, (L - 1, W, RW_MV_LORA), W ** -0.5),
        "rw_v2": n(23, (L - 1, RW_MV_LORA, W), RW_MV_LORA ** -0.5),
        "w_branch": n(24, (L, N_BRANCH, W, D_MODEL), W ** -0.5),
        "w_out": n(25, (L, D_MODEL, D_MODEL), D_MODEL ** -0.5),
        "ffn_w_up": n(26, (L, D_MODEL, 2 * D_FF), D_MODEL ** -0.5),
        "ffn_conv_w": n(27, (L, CONV_W, D_FF), CONV_W ** -0.5),
        "ffn_conv_b": n(28, (L, D_FF), 0.02),
        "ffn_w_down": n(29, (L, D_FF, D_MODEL), D_FF ** -0.5),
    }


def reference(x, norm_mix_w, norm_ffn_w, norm_final_w, w_in, hg_lb_table, hg_norm_w, gla_a_w2, gla_a_b,
              gla_norm_w, rw_mu, rw_w0, rw_w2, rw_a0, rw_a2, rw_g2, rw_k_k, rw_k_a, rw_r_k, rw_ln_w, rw_ln_b,
              rw_v0, rw_v1, rw_v2, w_branch, w_out, ffn_w_up, ffn_conv_w, ffn_conv_b, ffn_w_down):
    B, S, _ = x.shape
    pos = jnp.arange(S, dtype=jnp.int32)
    p_lb = jax.nn.softmax(hg_lb_table.astype(F32), axis=0)
    lower_bounds = jnp.cumsum(p_lb, axis=0) - p_lb[0:1]
    v_first = None
    for layer in range(DEPTH):
        h = rmsnorm(x, norm_mix_w[layer])
        z = h @ w_in[layer]
        z_hg, z_gla, z_rw, z_at, z_gate = _split(z, (HG_WIDTH, GLA_WIDTH, RW_WIDTH, 3 * AT_WIDTH, GATE_WIDTH))
        o_a = hgrn2_mixer(z_hg, lower_bounds[layer], hg_norm_w[layer])
        o_b = gla_mixer(z_gla, gla_a_w2[layer], gla_a_b[layer], gla_norm_w[layer])
        if layer == 0:
            o_c, v_first = rwkv7_mixer(z_rw, rw_mu[layer], rw_w0[layer], rw_w2[layer], rw_a0[layer], rw_a2[layer],
                                       rw_g2[layer], rw_k_k[layer], rw_k_a[layer], rw_r_k[layer], rw_ln_w[layer],
                                       rw_ln_b[layer], None, None, None, None)
        else:
            o_c, _ = rwkv7_mixer(z_rw, rw_mu[layer], rw_w0[layer], rw_w2[layer], rw_a0[layer], rw_a2[layer],
                                 rw_g2[layer], rw_k_k[layer], rw_k_a[layer], rw_r_k[layer], rw_ln_w[layer],
                                 rw_ln_b[layer], v_first, rw_v0[layer - 1], rw_v1[layer - 1], rw_v2[layer - 1])
        o_d = dilated_mixer(z_at, pos)
        branches = jnp.stack([o_a, o_b, o_c, o_d], axis=2)
        proj = jnp.einsum('bsnw,nwd->bsnd', branches, w_branch[layer])
        gates = jax.nn.sigmoid(z_gate.reshape(B, S, N_BRANCH, D_MODEL))
        x = x + jnp.sum(gates * proj, axis=2) @ w_out[layer]
        h2 = rmsnorm(x, norm_ffn_w[layer])
        up, gate_ff = _split(h2 @ ffn_w_up[layer], (D_FF, D_FF))
        up = causal_dwconv(up, ffn_conv_w[layer], ffn_conv_b[layer])
        x = x + (jax.nn.silu(up) * gate_ff) @ ffn_w_down[layer]
    return rmsnorm(x, norm_final_w)
```

```python
import functools

import numpy as np
import jax
import jax.numpy as jnp
from jax import lax
from jax.experimental import pallas as pl
from jax.experimental.pallas import tpu as pltpu

F32 = jnp.float32
BF16 = jnp.bfloat16

D_MODEL = 1024
DEPTH = 2
HEAD_DIM = 64
MIX_WIDTH = D_MODEL // 4
N_BRANCH = 4
NORM_EPS = 1e-6
MASK_VALUE = -1e9
EXP_CLIP = 60.0
N_HEADS = MIX_WIDTH // HEAD_DIM
GLA_DK = 32
GLA_RANK = 16
GLA_TAU = 16.0
RW_DECAY_LORA = 32
RW_AAA_LORA = 32
RW_MV_LORA = 32
RW_GATE_LORA = 64
RW_GN_EPS = 64e-5
DIL_PAIRS = ((128, 1), (512, 4), (2048, 16))
N_DIL = 3
AT_WIDTH = N_DIL * N_HEADS * HEAD_DIM
ROPE_THETA = 10000.0
D_FF = 11 * D_MODEL // 4
CONV_W = 3

HG_WIDTH = 4 * MIX_WIDTH
GLA_QK = N_HEADS * GLA_DK
GLA_WIDTH = 2 * GLA_QK + 2 * MIX_WIDTH + GLA_RANK
RW_WIDTH = 3 * MIX_WIDTH + RW_DECAY_LORA + RW_AAA_LORA + RW_GATE_LORA
GATE_WIDTH = N_BRANCH * D_MODEL

VMEM_LIMIT_BYTES = 56 * 1024 * 1024
ROW_TILE = 256
TIME_BLOCK = 512
GLA_CHUNK = 128
RW_CHUNK = 64
ATT_BLOCK = 128


def _const_spec(shape):
    nd = len(shape)
    return pl.BlockSpec(shape, lambda *_: (0,) * nd, pipeline_mode=pl.Buffered(1))


def _params(n_grid):
    return pltpu.CompilerParams(dimension_semantics=("arbitrary",) * n_grid, vmem_limit_bytes=VMEM_LIMIT_BYTES)


def _mm(a, b):
    return jnp.dot(a.astype(BF16), b.astype(BF16), preferred_element_type=F32)


def _mm_nt(a, b):
    return lax.dot_general(a.astype(BF16), b.astype(BF16), (((1,), (1,)), ((), ())), preferred_element_type=F32)


def _mm_tn(a, b):
    return lax.dot_general(a.astype(BF16), b.astype(BF16), (((0,), (0,)), ((), ())), preferred_element_type=F32)


def _split3(a):
    hi = a.astype(BF16)
    r1 = a - hi.astype(F32)
    mid = r1.astype(BF16)
    lo = (r1 - mid.astype(F32)).astype(BF16)
    return hi, mid, lo


def _mm_sel_lhs(sel, a):
    hi, mid, lo = _split3(a)
    return (jnp.dot(sel, hi, preferred_element_type=F32) + jnp.dot(sel, mid, preferred_element_type=F32)
            + jnp.dot(sel, lo, preferred_element_type=F32))


def _mm_sel_rhs(a, sel):
    hi, mid, lo = _split3(a)
    return (jnp.dot(hi, sel, preferred_element_type=F32) + jnp.dot(mid, sel, preferred_element_type=F32)
            + jnp.dot(lo, sel, preferred_element_type=F32))


def _rms(x, w):
    return x * lax.rsqrt(jnp.mean(x * x, axis=-1, keepdims=True) + NORM_EPS) * w


def _sigmoid(x):
    return jax.nn.sigmoid(x)


def _log_sigmoid(x):
    return jnp.minimum(x, 0.0) - jnp.log1p(jnp.exp(-jnp.abs(x)))


def _softplus(x):
    return jnp.maximum(x, 0.0) + jnp.log1p(jnp.exp(-jnp.abs(x)))


def _n_levels(chunk):
    return int(np.log2(chunk))


def _prefix_matrix(chunk):
    C, L = chunk, _n_levels(chunk)
    M = np.zeros(((L + 1) * C, C), np.float32)
    M[:C] = np.tril(np.ones((C, C), np.float32))
    for l in range(L):
        m = C >> (l + 1)
        for t in range(C):
            mid = (t // (2 * m)) * 2 * m + m - 1
            if t % (2 * m) >= m:
                M[(l + 1) * C + t, mid + 1:t + 1] = 1.0
            else:
                M[(l + 1) * C + t, t + 1:mid + 1] = 1.0
    return jnp.asarray(M, BF16)


def _block_ones(n_groups, w_in, w_out):
    return jnp.asarray(np.kron(np.eye(n_groups, dtype=np.float32), np.ones((w_in, w_out), np.float32)), BF16)


def _rope_tables(seq):
    half = HEAD_DIM // 2
    inv = ROPE_THETA ** (-jnp.arange(half, dtype=F32) / half)
    ang = jnp.arange(seq, dtype=jnp.int32).astype(F32)[:, None] * inv[None, :]
    cos, sin = jnp.cos(ang), jnp.sin(ang)
    cos_h = jnp.concatenate([cos, cos], axis=1)
    sin_h = jnp.concatenate([-sin, sin], axis=1)
    return jnp.tile(cos_h, (1, N_HEADS)), jnp.tile(sin_h, (1, N_HEADS))


def _inproj_kernel(x_ref, nw_ref, cos_ref, sin_ref, whg_ref, wgla_ref, wrw_ref, wat_ref,
                   zhg_ref, zgla_ref, zrw_ref, zat_ref):
    hb = _rms(x_ref[...], nw_ref[...]).astype(BF16)
    zhg_ref[...] = jnp.dot(hb, whg_ref[...], preferred_element_type=F32)
    zgla_ref[...] = jnp.dot(hb, wgla_ref[...], preferred_element_type=F32)
    zrw_ref[...] = jnp.dot(hb, wrw_ref[...], preferred_element_type=F32)
    zat = jnp.dot(hb, wat_ref[...], preferred_element_type=F32)
    cos, sin = cos_ref[...], sin_ref[...]
    lane = lax.broadcasted_iota(jnp.int32, cos.shape, 1)
    first_half = (lane % HEAD_DIM) < (HEAD_DIM // 2)
    for j in range(2 * N_DIL):
        p = zat[:, j * MIX_WIDTH:(j + 1) * MIX_WIDTH]
        partner = jnp.where(first_half, pltpu.roll(p, MIX_WIDTH - HEAD_DIM // 2, 1), pltpu.roll(p, HEAD_DIM // 2, 1))
        zat_ref[:, j * MIX_WIDTH:(j + 1) * MIX_WIDTH] = p * cos + partner * sin
    zat_ref[:, 2 * AT_WIDTH:] = zat[:, 2 * AT_WIDTH:]


def _inproj(x2, nw, cos_t, sin_t, w_hg, w_gla, w_rw, w_at, seq):
    T, D = x2.shape
    tm = ROW_TILE
    n_s = seq // tm
    row = lambda w: pl.BlockSpec((tm, w), lambda i: (i, 0))
    pos = pl.BlockSpec((tm, MIX_WIDTH), lambda i: (i % n_s, 0))
    widths = (HG_WIDTH, GLA_WIDTH, RW_WIDTH, 3 * AT_WIDTH)
    return pl.pallas_call(
        _inproj_kernel,
        grid=(T // tm,),
        in_specs=[row(D), _const_spec((1, D)), pos, pos,
                  _const_spec(w_hg.shape), _const_spec(w_gla.shape), _const_spec(w_rw.shape), _const_spec(w_at.shape)],
        out_specs=[row(w) for w in widths],
        out_shape=[jax.ShapeDtypeStruct((T, w), F32) for w in widths],
        compiler_params=_params(1),
        name="inproj",
    )(x2, nw, cos_t, sin_t, w_hg, w_gla, w_rw, w_at)


def _level_masks(chunk):
    t = lax.broadcasted_iota(jnp.int32, (chunk, chunk), 0)
    s = lax.broadcasted_iota(jnp.int32, (chunk, chunk), 1)
    masks = []
    for l in range(_n_levels(chunk)):
        m = chunk >> (l + 1)
        sh = int(np.log2(2 * m))
        masks.append(((t >> sh) == (s >> sh)) & ((t & m) != 0) & ((s & m) == 0))
    return masks


def _gla_chunk(q, k, v, g, st_ref, pm_ref, masks, qk_ones, *, chunk, dk, dv):
    C = chunk
    D = _mm_sel_lhs(pm_ref[...], g)
    b = D[:C]
    b_last = b[C - 1:C, :]
    q_in = q * jnp.exp(b)
    k_dec = k * jnp.exp(b_last - b)
    dec_last = jnp.exp(b_last)
    qf, kf = [], []
    for l in range(len(masks)):
        e = jnp.exp(D[(l + 1) * C:(l + 2) * C])
        qf.append((q * e).astype(BF16))
        kf.append((k * e).astype(BF16))
    diag = _mm(q * k, qk_ones)
    outs = []
    for h in range(N_HEADS):
        ks = slice(h * dk, (h + 1) * dk)
        vh = v[:, h * dv:(h + 1) * dv]
        sc = jnp.zeros((C, C), F32)
        for l in range(len(masks)):
            sc = jnp.where(masks[l], _mm_nt(qf[l][:, ks], kf[l][:, ks]), sc)
        st = st_ref[h]
        outs.append(_mm(sc, vh) + _mm_nt(q_in[:, ks], st))
        st_ref[h] = st * dec_last[:, ks] + _mm_tn(vh, k_dec[:, ks])
    return jnp.concatenate(outs, axis=1) + diag * v


def _head_rmsnorm(o, head_ones, w):
    ms = _mm_sel_rhs(o * o, head_ones) * (1.0 / HEAD_DIM)
    return o * lax.rsqrt(ms + NORM_EPS) * w


def _hgrn2_kernel(z_ref, lbt_ref, nw_ref, pm_ref, ones_ref, o_ref, st_ref, *, layer, n_chunks):
    @pl.when(pl.program_id(1) == 0)
    def _():
        st_ref[...] = jnp.zeros_like(st_ref)

    tab = lbt_ref[...]
    e = jnp.exp(tab - jnp.max(tab, axis=0, keepdims=True))
    p = e / jnp.sum(e, axis=0, keepdims=True)
    lb = jnp.zeros((1, MIX_WIDTH), F32)
    for i in range(1, layer + 1):
        lb = lb + p[i:i + 1, :]
    masks = _level_masks(GLA_CHUNK)
    ones = ones_ref[...]

    def body(c, carry):
        r0 = pl.multiple_of(c * GLA_CHUNK, GLA_CHUNK)
        z = z_ref[pl.ds(r0, GLA_CHUNK), :]
        q_raw, fz = z[:, :MIX_WIDTH], z[:, MIX_WIDTH:2 * MIX_WIDTH]
        i_raw, g_raw = z[:, 2 * MIX_WIDTH:3 * MIX_WIDTH], z[:, 3 * MIX_WIDTH:]
        log_f = _log_sigmoid(fz) + jnp.log1p(lb * jnp.exp(jnp.minimum(-fz, EXP_CLIP)))
        k = (1.0 - lb) * _sigmoid(-fz)
        q = q_raw * _sigmoid(q_raw)
        o = _gla_chunk(q, k, i_raw, log_f, st_ref, pm_ref, masks, ones, chunk=GLA_CHUNK, dk=HEAD_DIM, dv=HEAD_DIM)
        o_ref[pl.ds(r0, GLA_CHUNK), :] = _head_rmsnorm(o, ones, nw_ref[...]) * _sigmoid(g_raw)
        return carry

    lax.fori_loop(0, n_chunks, body, 0)


def _gla_kernel(z_ref, aw2_ref, ab_ref, nw_ref, pm_ref, qk_ones_ref, ones_ref, o_ref, st_ref, *, n_chunks):
    @pl.when(pl.program_id(1) == 0)
    def _():
        st_ref[...] = jnp.zeros_like(st_ref)

    masks = _level_masks(GLA_CHUNK)

    def body(c, carry):
        r0 = pl.multiple_of(c * GLA_CHUNK, GLA_CHUNK)
        z = z_ref[pl.ds(r0, GLA_CHUNK), :]
        q = z[:, :GLA_QK] * (GLA_DK ** -0.5)
        k = z[:, GLA_QK:2 * GLA_QK]
        v = z[:, 2 * GLA_QK:2 * GLA_QK + MIX_WIDTH]
        gate = z[:, 2 * GLA_QK + MIX_WIDTH:2 * GLA_QK + 2 * MIX_WIDTH]
        a_low = z[:, 2 * GLA_QK + 2 * MIX_WIDTH:]
        log_alpha = _log_sigmoid(_mm(a_low, aw2_ref[...]) + ab_ref[...]) * (1.0 / GLA_TAU)
        o = _gla_chunk(q, k, v, log_alpha, st_ref, pm_ref, masks, qk_ones_ref[...],
                       chunk=GLA_CHUNK, dk=GLA_DK, dv=HEAD_DIM)
        o_ref[pl.ds(r0, GLA_CHUNK), :] = _head_rmsnorm(o, ones_ref[...], nw_ref[...]) * (gate * _sigmoid(gate))
        return carry

    lax.fori_loop(0, n_chunks, body, 0)


def _time_grid(T, seq):
    ts = TIME_BLOCK
    n_t = seq // ts
    return ts, n_t, (T // seq, n_t)


def _hgrn2(z_hg, lb_table, norm_w, layer, seq):
    T = z_hg.shape[0]
    ts, n_t, grid = _time_grid(T, seq)
    blk = lambda w: pl.BlockSpec((ts, w), lambda b, i: (b * n_t + i, 0))
    pm = _prefix_matrix(GLA_CHUNK)
    ones = _block_ones(N_HEADS, HEAD_DIM, HEAD_DIM)
    return pl.pallas_call(
        functools.partial(_hgrn2_kernel, layer=layer, n_chunks=ts // GLA_CHUNK),
        grid=grid,
        in_specs=[blk(HG_WIDTH), _const_spec(lb_table.shape), _const_spec((1, MIX_WIDTH)),
                  _const_spec(pm.shape), _const_spec(ones.shape)],
        out_specs=blk(MIX_WIDTH),
        out_shape=jax.ShapeDtypeStruct((T, MIX_WIDTH), F32),
        scratch_shapes=[pltpu.VMEM((N_HEADS, HEAD_DIM, HEAD_DIM), F32)],
        compiler_params=_params(2),
        name="hgrn2",
    )(z_hg, lb_table, norm_w, pm, ones)


def _gla(z_gla, a_w2, a_b, norm_w, seq):
    T = z_gla.shape[0]
    ts, n_t, grid = _time_grid(T, seq)
    blk = lambda w: pl.BlockSpec((ts, w), lambda b, i: (b * n_t + i, 0))
    pm = _prefix_matrix(GLA_CHUNK)
    qk_ones = _block_ones(N_HEADS, GLA_DK, HEAD_DIM)
    ones = _block_ones(N_HEADS, HEAD_DIM, HEAD_DIM)
    return pl.pallas_call(
        functools.partial(_gla_kernel, n_chunks=ts // GLA_CHUNK),
        grid=grid,
        in_specs=[blk(GLA_WIDTH), _const_spec(a_w2.shape), _const_spec((1, GLA_QK)), _const_spec((1, MIX_WIDTH)),
                  _const_spec(pm.shape), _const_spec(qk_ones.shape), _const_spec(ones.shape)],
        out_specs=blk(MIX_WIDTH),
        out_shape=jax.ShapeDtypeStruct((T, MIX_WIDTH), F32),
        scratch_shapes=[pltpu.VMEM((N_HEADS, HEAD_DIM, GLA_DK), F32)],
        compiler_params=_params(2),
        name="gla",
    )(z_gla, a_w2, a_b, norm_w, pm, qk_ones, ones)


_RW_R, _RW_K, _RW_V = 0, MIX_WIDTH, 2 * MIX_WIDTH
_RW_WL = 3 * MIX_WIDTH
_RW_AL = _RW_WL + RW_DECAY_LORA
_RW_GL = _RW_AL + RW_AAA_LORA


def _rwkv_kernel(*refs, has_vfirst, n_chunks):
    if has_vfirst:
        (z_ref, vf_ref, mu_ref, w0_ref, w2_ref, a0_ref, a2_ref, g2_ref, kk_ref, ka_ref, rk_ref, lnw_ref, lnb_ref,
         v0_ref, v1_ref, v2_ref, tri_ref, ones_ref, o_ref, s_ref, carry_ref, zs_ref) = refs
    else:
        (z_ref, mu_ref, w0_ref, w2_ref, a0_ref, a2_ref, g2_ref, kk_ref, ka_ref, rk_ref, lnw_ref, lnb_ref,
         tri_ref, ones_ref, o_ref, vout_ref, s_ref, carry_ref, zs_ref) = refs
    C = RW_CHUNK
    ts = z_ref.shape[0]

    @pl.when(pl.program_id(1) == 0)
    def _():
        s_ref[...] = jnp.zeros_like(s_ref)
        carry_ref[...] = jnp.zeros_like(carry_ref)

    z = z_ref[...]
    zp = pltpu.roll(z, 1, 0)
    zp = jnp.where(lax.broadcasted_iota(jnp.int32, z.shape, 0) == 0, carry_ref[...], zp)
    carry_ref[...] = z[ts - 1:ts, :]
    zs_ref[...] = z + mu_ref[...] * (zp - z)

    ti = lax.broadcasted_iota(jnp.int32, (C, C), 0)
    si = lax.broadcasted_iota(jnp.int32, (C, C), 1)
    incl, strict = si <= ti, si < ti
    eye = (si == ti).astype(F32)
    ones = ones_ref[...]
    gsum = lambda t: _mm_sel_rhs(t, ones)

    def body(c, carry):
        r0 = pl.multiple_of(c * C, C)
        zc = zs_ref[pl.ds(r0, C), :]
        r, k, v = zc[:, _RW_R:_RW_K], zc[:, _RW_K:_RW_V], zc[:, _RW_V:_RW_WL]
        wl, al, gl = zc[:, _RW_WL:_RW_AL], zc[:, _RW_AL:_RW_GL], zc[:, _RW_GL:]
        w = -_softplus(-(w0_ref[...] + _mm(jnp.tanh(wl), w2_ref[...]))) - 0.5
        lw = -jnp.exp(w)
        a = _sigmoid(a0_ref[...] + _mm(al, a2_ref[...]))
        g = _mm(_sigmoid(gl), g2_ref[...])
        if has_vfirst:
            vf = vf_ref[pl.ds(r0, C), :]
            v = v + (vf - v) * _sigmoid(v0_ref[...] + _mm(_mm(v, v1_ref[...]), v2_ref[...]))
        else:
            vout_ref[pl.ds(r0, C), :] = v
        kk = k * kk_ref[...]
        kk = kk / jnp.maximum(jnp.sqrt(gsum(kk * kk)), 1e-12)
        k_mod = k * (1.0 + (a - 1.0) * ka_ref[...])
        kka = kk * a

        cum = _mm_sel_lhs(tri_ref[...], lw)
        cum_last = cum[C - 1:C, :]
        r_t = r * jnp.exp(cum)
        a_t = -kk * jnp.exp(cum - lw)
        inv = jnp.exp(-cum)
        k_h, b_h = k_mod * inv, kka * inv
        end = jnp.exp(cum_last - cum)
        k_end, b_end = k_mod * end, kka * end
        w_end = jnp.exp(cum_last)

        ys = []
        for h in range(N_HEADS):
            hs = slice(h * HEAD_DIM, (h + 1) * HEAD_DIM)
            aa = _mm_nt(jnp.concatenate([r_t[:, hs], a_t[:, hs]], axis=0),
                        jnp.concatenate([k_h[:, hs], b_h[:, hs]], axis=0))
            a_rk = jnp.where(incl, aa[:C, :C], 0.0)
            a_rb = jnp.where(incl, aa[:C, C:], 0.0)
            a_ak = jnp.where(strict, aa[C:, :C], 0.0)
            n = jnp.where(strict, aa[C:, C:], 0.0)
            inv_t = eye + n
            npow = n
            for _ in range(_n_levels(C) - 1):
                npow = _mm(npow, npow)
                inv_t = inv_t + _mm(npow, inv_t)
            vh = v[:, hs]
            s = s_ref[h]
            u = _mm(inv_t, _mm_nt(a_t[:, hs], s) + _mm(a_ak, vh))
            ys.append(_mm_nt(r_t[:, hs], s) + _mm(a_rk, vh) + _mm(a_rb, u))
            s_ref[h] = s * w_end[:, hs] + _mm_tn(vh, k_end[:, hs]) + _mm_tn(u, b_end[:, hs])
        y = jnp.concatenate(ys, axis=1)

        mean = gsum(y) * (1.0 / HEAD_DIM)
        yc = y - mean
        var = gsum(yc * yc) * (1.0 / HEAD_DIM)
        yn = yc * lax.rsqrt(var + RW_GN_EPS) * lnw_ref[...] + lnb_ref[...]
        bonus = gsum(r * k_mod * rk_ref[...]) * v
        o_ref[pl.ds(r0, C), :] = (yn + bonus) * g
        return carry

    lax.fori_loop(0, n_chunks, body, 0)


def _rwkv(z_rw, v_first, p, seq):
    T = z_rw.shape[0]
    ts, n_t, grid = _time_grid(T, seq)
    blk = lambda w: pl.BlockSpec((ts, w), lambda b, i: (b * n_t + i, 0))
    tri = jnp.asarray(np.tril(np.ones((RW_CHUNK, RW_CHUNK), np.float32)), BF16)
    ones = _block_ones(N_HEADS, HEAD_DIM, HEAD_DIM)
    has_vfirst = v_first is not None
    names = ["mu", "w0", "w2", "a0", "a2", "g2", "k_k", "k_a", "r_k", "ln_w", "ln_b"]
    args = [z_rw] + ([v_first] if has_vfirst else []) + [p[n] for n in names]
    specs = [blk(RW_WIDTH)] + ([blk(MIX_WIDTH)] if has_vfirst else []) + [_const_spec(p[n].shape) for n in names]
    if has_vfirst:
        args += [p["v0"], p["v1"], p["v2"]]
        specs += [_const_spec(p[n].shape) for n in ("v0", "v1", "v2")]
    args += [tri, ones]
    specs += [_const_spec(tri.shape), _const_spec(ones.shape)]
    out_sds = jax.ShapeDtypeStruct((T, MIX_WIDTH), F32)
    res = pl.pallas_call(
        functools.partial(_rwkv_kernel, has_vfirst=has_vfirst, n_chunks=ts // RW_CHUNK),
        grid=grid,
        in_specs=specs,
        out_specs=blk(MIX_WIDTH) if has_vfirst else [blk(MIX_WIDTH), blk(MIX_WIDTH)],
        out_shape=out_sds if has_vfirst else [out_sds, out_sds],
        scratch_shapes=[pltpu.VMEM((N_HEADS, HEAD_DIM, HEAD_DIM), F32), pltpu.VMEM((1, RW_WIDTH), F32),
                        pltpu.VMEM((ts, RW_WIDTH), F32)],
        compiler_params=_params(2),
        name="rwkv7",
    )(*args)
    return (res, None) if has_vfirst else (res[0], res[1])


def _attn_kernel(q_ref, kp_ref, kc_ref, vp_ref, vc_ref, o_ref, lse_ref):
    qb = ATT_BLOCK
    i = pl.program_id(2)
    q = q_ref[0] * (HEAD_DIM ** -0.5)
    kp, kc, vp, vc = kp_ref[0], kc_ref[0], vp_ref[0], vc_ref[0]
    row = lax.broadcasted_iota(jnp.int32, (qb, qb), 0)
    col = lax.broadcasted_iota(jnp.int32, (qb, qb), 1)
    mask_prev = col >= row + jnp.where(i > 0, 0, qb)
    mask_cur = col <= row
    for h in range(N_HEADS):
        hs = slice(h * HEAD_DIM, (h + 1) * HEAD_DIM)
        s_p = jnp.where(mask_prev, _mm_nt(q[:, hs], kp[:, hs]), MASK_VALUE)
        s_c = jnp.where(mask_cur, _mm_nt(q[:, hs], kc[:, hs]), MASK_VALUE)
        m = jnp.maximum(jnp.max(s_p, axis=-1, keepdims=True), jnp.max(s_c, axis=-1, keepdims=True))
        p_p, p_c = jnp.exp(s_p - m), jnp.exp(s_c - m)
        l = jnp.sum(p_p, axis=-1, keepdims=True) + jnp.sum(p_c, axis=-1, keepdims=True)
        o_ref[0, :, hs] = (_mm(p_p, vp[:, hs]) + _mm(p_c, vc[:, hs])) / l
        lse_ref[0, :, hs] = jnp.broadcast_to(m + jnp.log(l), (qb, HEAD_DIM))


def _attention_group(z_at, g, dil, batch, seq):
    T = z_at.shape[0]
    n_sub = seq // dil
    nq = n_sub // ATT_BLOCK
    z3 = z_at.reshape(batch, n_sub, dil * 3 * AT_WIDTH)
    per_pos = 3 * AT_WIDTH // MIX_WIDTH

    def spec(col, prev):
        if prev:
            return pl.BlockSpec((1, ATT_BLOCK, MIX_WIDTH), lambda b, r, i: (b, jnp.maximum(i - 1, 0), r * per_pos + col))
        return pl.BlockSpec((1, ATT_BLOCK, MIX_WIDTH), lambda b, r, i: (b, i, r * per_pos + col))

    out_spec = pl.BlockSpec((1, ATT_BLOCK, MIX_WIDTH), lambda b, r, i: (b, i, r))
    sds = jax.ShapeDtypeStruct((batch, n_sub, dil * MIX_WIDTH), F32)
    o, lse = pl.pallas_call(
        _attn_kernel,
        grid=(batch, dil, nq),
        in_specs=[spec(g, False), spec(N_DIL + g, True), spec(N_DIL + g, False),
                  spec(2 * N_DIL + g, True), spec(2 * N_DIL + g, False)],
        out_specs=[out_spec, out_spec],
        out_shape=[sds, sds],
        compiler_params=_params(3),
        name=f"dilated_attn_{dil}",
    )(z3, z3, z3, z3, z3)
    return o.reshape(T, MIX_WIDTH), lse.reshape(T, MIX_WIDTH)


def _merge_kernel(x_ref, nw_ref, wg_ref, oa_ref, ob_ref, oc_ref, o0_ref, l0_ref, o1_ref, l1_ref, o2_ref, l2_ref,
                  wb_ref, wo_ref, out_ref):
    x = x_ref[...]
    hb = _rms(x, nw_ref[...]).astype(BF16)
    l0, l1, l2 = l0_ref[...], l1_ref[...], l2_ref[...]
    m = jnp.maximum(jnp.maximum(l0, l1), l2)
    e0, e1, e2 = jnp.exp(l0 - m), jnp.exp(l1 - m), jnp.exp(l2 - m)
    o_d = (e0 * o0_ref[...] + e1 * o1_ref[...] + e2 * o2_ref[...]) / (e0 + e1 + e2)
    acc = jnp.zeros(x.shape, F32)
    for kbr, o in enumerate((oa_ref[...], ob_ref[...], oc_ref[...], o_d)):
        gate = _sigmoid(jnp.dot(hb, wg_ref[:, kbr * D_MODEL:(kbr + 1) * D_MODEL], preferred_element_type=F32))
        acc = acc + gate * _mm(o, wb_ref[kbr])
    out_ref[...] = x + _mm(acc, wo_ref[...])


def _merge(x2, nw, w_gate, o_a, o_b, o_c, att, w_branch, w_out):
    T, D = x2.shape
    tm = ROW_TILE
    row = lambda w: pl.BlockSpec((tm, w), lambda i: (i, 0))
    return pl.pallas_call(
        _merge_kernel,
        grid=(T // tm,),
        in_specs=[row(D), _const_spec((1, D)), _const_spec(w_gate.shape)] + [row(MIX_WIDTH)] * 9
                 + [_const_spec(w_branch.shape), _const_spec(w_out.shape)],
        out_specs=row(D),
        out_shape=jax.ShapeDtypeStruct((T, D), F32),
        compiler_params=_params(1),
        name="merge",
    )(x2, nw, w_gate, o_a, o_b, o_c, *att, w_branch, w_out)


def _ffn_kernel(x_ref, nw_ref, wup_ref, cw_ref, cb_ref, wdn_ref, fw_ref, out_ref, tail_ref, *, n_s, final_norm):
    tm = x_ref.shape[0]

    @pl.when(pl.program_id(0) % n_s == 0)
    def _():
        tail_ref[...] = jnp.zeros_like(tail_ref)

    x = x_ref[...]
    hb = _rms(x, nw_ref[...]).astype(BF16)
    up = jnp.dot(hb, wup_ref[:, :D_FF], preferred_element_type=F32)
    gate = jnp.dot(hb, wup_ref[:, D_FF:], preferred_element_type=F32)
    row = lax.broadcasted_iota(jnp.int32, up.shape, 0)
    tail = tail_ref[...]
    up1 = jnp.where(row == 0, tail[1:2, :], pltpu.roll(up, 1, 0))
    up2 = jnp.where(row == 0, tail[0:1, :], jnp.where(row == 1, tail[1:2, :], pltpu.roll(up, 2, 0)))
    tail_ref[...] = up[tm - 2:tm, :]
    cw = cw_ref[...]
    conv = cb_ref[...] + cw[0:1, :] * up + cw[1:2, :] * up1 + cw[2:3, :] * up2
    act = conv * _sigmoid(conv) * gate
    y = x + _mm(act, wdn_ref[...])
    out_ref[...] = _rms(y, fw_ref[...]) if final_norm else y


def _ffn(x2, nw, w_up, conv_w, conv_b, w_down, final_w, seq, final_norm):
    T, D = x2.shape
    tm = ROW_TILE
    row = pl.BlockSpec((tm, D), lambda i: (i, 0))
    return pl.pallas_call(
        functools.partial(_ffn_kernel, n_s=seq // tm, final_norm=final_norm),
        grid=(T // tm,),
        in_specs=[row, _const_spec((1, D)), _const_spec(w_up.shape), _const_spec(conv_w.shape),
                  _const_spec((1, D_FF)), _const_spec(w_down.shape), _const_spec((1, D))],
        out_specs=row,
        out_shape=jax.ShapeDtypeStruct((T, D), F32),
        scratch_shapes=[pltpu.VMEM((CONV_W - 1, D_FF), F32)],
        compiler_params=_params(1),
        name="convglu",
    )(x2, nw, w_up, conv_w, conv_b, w_down, final_w)


def kernel(x, norm_mix_w, norm_ffn_w, norm_final_w, w_in, hg_lb_table, hg_norm_w, gla_a_w2, gla_a_b, gla_norm_w, rw_mu, rw_w0, rw_w2, rw_a0, rw_a2, rw_g2, rw_k_k, rw_k_a, rw_r_k, rw_ln_w, rw_ln_b, rw_v0, rw_v1, rw_v2, w_branch, w_out, ffn_w_up, ffn_conv_w, ffn_conv_b, ffn_w_down):
    B, S, D = x.shape
    T = B * S
    x2 = x.reshape(T, D)
    cos_t, sin_t = _rope_tables(S)
    row = lambda t: t.reshape(1, -1)
    W = MIX_WIDTH
    src = np.concatenate([np.arange(0, W), np.arange(W + RW_DECAY_LORA, 3 * W + RW_DECAY_LORA),
                          np.arange(W, W + RW_DECAY_LORA), np.arange(3 * W + RW_DECAY_LORA, RW_WIDTH)])
    o_hg, o_gla, o_rw, o_at = 0, HG_WIDTH, HG_WIDTH + GLA_WIDTH, HG_WIDTH + GLA_WIDTH + RW_WIDTH
    o_gate = o_at + 3 * AT_WIDTH
    v_first = None
    for layer in range(DEPTH):
        w = w_in[layer]
        w_hg = w[:, o_hg:o_gla].astype(BF16)
        w_gla = w[:, o_gla:o_rw].astype(BF16)
        w_rw = w[:, o_rw:o_at][:, src].astype(BF16)
        w_at = w[:, o_at:o_gate].astype(BF16)
        w_gate = w[:, o_gate:].astype(BF16)
        z_hg, z_gla, z_rw, z_at = _inproj(x2, row(norm_mix_w[layer]), cos_t, sin_t, w_hg, w_gla, w_rw, w_at, S)

        o_a = _hgrn2(z_hg, hg_lb_table, row(hg_norm_w[layer]), layer, S)
        o_b = _gla(z_gla, gla_a_w2[layer].astype(BF16), row(gla_a_b[layer]), row(gla_norm_w[layer]), S)
        p = {"mu": row(rw_mu[layer][src]), "w0": row(rw_w0[layer]), "w2": rw_w2[layer].astype(BF16),
             "a0": row(rw_a0[layer]), "a2": rw_a2[layer].astype(BF16), "g2": rw_g2[layer].astype(BF16),
             "k_k": row(rw_k_k[layer]), "k_a": row(rw_k_a[layer]), "r_k": row(rw_r_k[layer]),
             "ln_w": row(rw_ln_w[layer]), "ln_b": row(rw_ln_b[layer])}
        if layer > 0:
            p.update(v0=row(rw_v0[layer - 1]), v1=rw_v1[layer - 1].astype(BF16), v2=rw_v2[layer - 1].astype(BF16))
        o_c, v_new = _rwkv(z_rw, v_first, p, S)
        if layer == 0:
            v_first = v_new
        att = []
        for g, (_, dil) in enumerate(DIL_PAIRS):
            att.extend(_attention_group(z_at, g, dil, B, S))

        x2 = _merge(x2, row(norm_mix_w[layer]), w_gate, o_a, o_b, o_c, att,
                    w_branch[layer].astype(BF16), w_out[layer].astype(BF16))
        x2 = _ffn(x2, row(norm_ffn_w[layer]), ffn_w_up[layer].astype(BF16), ffn_conv_w[layer],
                  row(ffn_conv_b[layer]), ffn_w_down[layer].astype(BF16), row(norm_final_w), S,
                  final_norm=(layer == DEPTH - 1))
    return x2.reshape(B, S, D)
```

```python
import functools

import numpy as np
import jax
import jax.numpy as jnp
from jax import lax
from jax.experimental import pallas as pl
from jax.experimental.pallas import tpu as pltpu

F32 = jnp.float32
BF16 = jnp.bfloat16

D_MODEL = 1024
DEPTH = 2
HEAD_DIM = 64
MIX_WIDTH = D_MODEL // 4
N_BRANCH = 4
NORM_EPS = 1e-6
MASK_VALUE = -1e9
EXP_CLIP = 60.0
N_HEADS = MIX_WIDTH // HEAD_DIM
GLA_DK = 32
GLA_RANK = 16
GLA_TAU = 16.0
RW_DECAY_LORA = 32
RW_AAA_LORA = 32
RW_MV_LORA = 32
RW_GATE_LORA = 64
RW_GN_EPS = 64e-5
DIL_PAIRS = ((128, 1), (512, 4), (2048, 16))
N_DIL = 3
AT_WIDTH = N_DIL * N_HEADS * HEAD_DIM
ROPE_THETA = 10000.0
D_FF = 11 * D_MODEL // 4
CONV_W = 3

HG_WIDTH = 4 * MIX_WIDTH
GLA_QK = N_HEADS * GLA_DK
GLA_WIDTH = 2 * GLA_QK + 2 * MIX_WIDTH + GLA_RANK
RW_WIDTH = 3 * MIX_WIDTH + RW_DECAY_LORA + RW_AAA_LORA + RW_GATE_LORA
GATE_WIDTH = N_BRANCH * D_MODEL

VMEM_LIMIT_BYTES = 56 * 1024 * 1024
ROW_TILE = 256
TIME_BLOCK = 512
GLA_CHUNK = 128
RW_CHUNK = 64
ATT_BLOCK = 128


def _const_spec(shape):
    nd = len(shape)
    return pl.BlockSpec(shape, lambda *_: (0,) * nd, pipeline_mode=pl.Buffered(1))


def _params(n_grid):
    return pltpu.CompilerParams(dimension_semantics=("arbitrary",) * n_grid, vmem_limit_bytes=VMEM_LIMIT_BYTES)


def _mm(a, b):
    return jnp.dot(a.astype(BF16), b.astype(BF16), preferred_element_type=F32)


def _mm_nt(a, b):
    return lax.dot_general(a.astype(BF16), b.astype(BF16), (((1,), (1,)), ((), ())), preferred_element_type=F32)


def _mm_tn(a, b):
    return lax.dot_general(a.astype(BF16), b.astype(BF16), (((0,), (0,)), ((), ())), preferred_element_type=F32)


def _split2(a):
    hi = a.astype(BF16)
    lo = (a - hi.astype(F32)).astype(BF16)
    return hi, lo


def _mm_sel_lhs(sel, a):
    hi, lo = _split2(a)
    return jnp.dot(sel, hi, preferred_element_type=F32) + jnp.dot(sel, lo, preferred_element_type=F32)


def _mm_sel_rhs(a, sel):
    hi, lo = _split2(a)
    return jnp.dot(hi, sel, preferred_element_type=F32) + jnp.dot(lo, sel, preferred_element_type=F32)


def _rms(x, w):
    return x * lax.rsqrt(jnp.mean(x * x, axis=-1, keepdims=True) + NORM_EPS) * w


def _sigmoid(x):
    return jax.nn.sigmoid(x)


def _log_sigmoid(x):
    return jnp.minimum(x, 0.0) - jnp.log1p(jnp.exp(-jnp.abs(x)))


def _softplus(x):
    return jnp.maximum(x, 0.0) + jnp.log1p(jnp.exp(-jnp.abs(x)))


def _n_levels(chunk):
    return int(np.log2(chunk))


def _prefix_matrix(chunk):
    C, L = chunk, _n_levels(chunk)
    M = np.zeros(((L + 1) * C, C), np.float32)
    M[:C] = np.tril(np.ones((C, C), np.float32))
    for l in range(L):
        m = C >> (l + 1)
        for t in range(C):
            mid = (t // (2 * m)) * 2 * m + m - 1
            if t % (2 * m) >= m:
                M[(l + 1) * C + t, mid + 1:t + 1] = 1.0
            else:
                M[(l + 1) * C + t, t + 1:mid + 1] = 1.0
    return jnp.asarray(M, BF16)


def _block_ones(n_groups, w_in, w_out):
    return jnp.asarray(np.kron(np.eye(n_groups, dtype=np.float32), np.ones((w_in, w_out), np.float32)), BF16)


def _rope_tables(seq):
    half = HEAD_DIM // 2
    inv = ROPE_THETA ** (-jnp.arange(half, dtype=F32) / half)
    ang = jnp.arange(seq, dtype=jnp.int32).astype(F32)[:, None] * inv[None, :]
    cos, sin = jnp.cos(ang), jnp.sin(ang)
    cos_h = jnp.concatenate([cos, cos], axis=1)
    sin_h = jnp.concatenate([-sin, sin], axis=1)
    return jnp.tile(cos_h, (1, N_HEADS)), jnp.tile(sin_h, (1, N_HEADS))


def _inproj_kernel(x_ref, nw_ref, cos_ref, sin_ref, whg_ref, wgla_ref, wrw_ref, wat_ref,
                   zhg_ref, zgla_ref, zrw_ref, zat_ref):
    hb = _rms(x_ref[...], nw_ref[...]).astype(BF16)
    zhg_ref[...] = jnp.dot(hb, whg_ref[...], preferred_element_type=F32)
    zgla_ref[...] = jnp.dot(hb, wgla_ref[...], preferred_element_type=F32)
    zrw_ref[...] = jnp.dot(hb, wrw_ref[...], preferred_element_type=F32)
    zat = jnp.dot(hb, wat_ref[...], preferred_element_type=F32)
    cos, sin = cos_ref[...], sin_ref[...]
    lane = lax.broadcasted_iota(jnp.int32, cos.shape, 1)
    first_half = (lane % HEAD_DIM) < (HEAD_DIM // 2)
    for j in range(2 * N_DIL):
        p = zat[:, j * MIX_WIDTH:(j + 1) * MIX_WIDTH]
        partner = jnp.where(first_half, pltpu.roll(p, MIX_WIDTH - HEAD_DIM // 2, 1), pltpu.roll(p, HEAD_DIM // 2, 1))
        zat_ref[:, j * MIX_WIDTH:(j + 1) * MIX_WIDTH] = p * cos + partner * sin
    zat_ref[:, 2 * AT_WIDTH:] = zat[:, 2 * AT_WIDTH:]


def _inproj(x2, nw, cos_t, sin_t, w_hg, w_gla, w_rw, w_at, seq):
    T, D = x2.shape
    tm = ROW_TILE
    n_s = seq // tm
    row = lambda w: pl.BlockSpec((tm, w), lambda i: (i, 0))
    pos = pl.BlockSpec((tm, MIX_WIDTH), lambda i: (i % n_s, 0))
    widths = (HG_WIDTH, GLA_WIDTH, RW_WIDTH, 3 * AT_WIDTH)
    return pl.pallas_call(
        _inproj_kernel,
        grid=(T // tm,),
        in_specs=[row(D), _const_spec((1, D)), pos, pos,
                  _const_spec(w_hg.shape), _const_spec(w_gla.shape), _const_spec(w_rw.shape), _const_spec(w_at.shape)],
        out_specs=[row(w) for w in widths],
        out_shape=[jax.ShapeDtypeStruct((T, w), F32) for w in widths],
        compiler_params=_params(1),
        name="inproj",
    )(x2, nw, cos_t, sin_t, w_hg, w_gla, w_rw, w_at)


def _level_masks(chunk):
    t = lax.broadcasted_iota(jnp.int32, (chunk, chunk), 0)
    s = lax.broadcasted_iota(jnp.int32, (chunk, chunk), 1)
    masks = []
    for l in range(_n_levels(chunk)):
        m = chunk >> (l + 1)
        sh = int(np.log2(2 * m))
        masks.append(((t >> sh) == (s >> sh)) & ((t & m) != 0) & ((s & m) == 0))
    return masks


def _gla_block(q, k, v, g, st_ref, pm_ref, qk_ones, *, chunk, dk, dv):
    C = chunk
    nc = q.shape[0] // C
    n_lvl = _n_levels(C)
    masks = _level_masks(C)
    pm = pm_ref[...]
    rows = lambda t, c: t[c * C:(c + 1) * C]
    stack = lambda f: jnp.concatenate([f(c) for c in range(nc)], axis=0)
    D = [_mm_sel_lhs(pm, rows(g, c)) for c in range(nc)]
    b = stack(lambda c: D[c][:C])
    b_last = stack(lambda c: jnp.broadcast_to(D[c][C - 1:C, :], (C, b.shape[1])))
    dec_last = [jnp.exp(D[c][C - 1:C, :]) for c in range(nc)]
    q_in = (q * jnp.exp(b)).astype(BF16)
    k_dec = (k * jnp.exp(b_last - b)).astype(BF16)
    vb = v.astype(BF16)
    qf, kf = [], []
    for l in range(n_lvl):
        e = jnp.exp(stack(lambda c: D[c][(l + 1) * C:(l + 2) * C]))
        qf.append((q * e).astype(BF16))
        kf.append((k * e).astype(BF16))
    diag = _mm(q * k, qk_ones)
    pairs = [(c, h) for c in range(nc) for h in range(N_HEADS)]
    bk = lambda t, c, h: t[c * C:(c + 1) * C, h * dk:(h + 1) * dk]
    bv = lambda t, c, h: t[c * C:(c + 1) * C, h * dv:(h + 1) * dv]
    nt = (((1,), (1,)), ((), ()))
    tn = (((0,), (0,)), ((), ()))
    prod = {(p, l): lax.dot_general(bk(qf[l], *p), bk(kf[l], *p), nt, preferred_element_type=F32)
            for p in pairs for l in range(n_lvl)}
    o_intra, kv = {}, {}
    for p in pairs:
        sc = jnp.zeros((C, C), F32)
        for l in range(n_lvl):
            sc = jnp.where(masks[l], prod[p, l], sc)
        o_intra[p] = jnp.dot(sc.astype(BF16), bv(vb, *p), preferred_element_type=F32)
        kv[p] = lax.dot_general(bv(vb, *p), bk(k_dec, *p), tn, preferred_element_type=F32)
    s = [st_ref[h] for h in range(N_HEADS)]
    out_rows = []
    for c in range(nc):
        o_inter = [lax.dot_general(bk(q_in, c, h), s[h].astype(BF16), nt, preferred_element_type=F32)
                   for h in range(N_HEADS)]
        s = [s[h] * dec_last[c][:, h * dk:(h + 1) * dk] + kv[c, h] for h in range(N_HEADS)]
        out_rows.append(jnp.concatenate([o_intra[c, h] + o_inter[h] for h in range(N_HEADS)], axis=1))
    for h in range(N_HEADS):
        st_ref[h] = s[h]
    return jnp.concatenate(out_rows, axis=0) + diag * v


def _head_rmsnorm(o, head_ones, w):
    ms = _mm_sel_rhs(o * o, head_ones) * (1.0 / HEAD_DIM)
    return o * lax.rsqrt(ms + NORM_EPS) * w


def _hgrn2_kernel(z_ref, lbt_ref, nw_ref, pm_ref, ones_ref, o_ref, st_ref, *, layer):
    @pl.when(pl.program_id(1) == 0)
    def _():
        st_ref[...] = jnp.zeros_like(st_ref)

    tab = lbt_ref[...]
    e = jnp.exp(tab - jnp.max(tab, axis=0, keepdims=True))
    p = e / jnp.sum(e, axis=0, keepdims=True)
    lb = jnp.zeros((1, MIX_WIDTH), F32)
    for i in range(1, layer + 1):
        lb = lb + p[i:i + 1, :]
    ones = ones_ref[...]
    z = z_ref[...]
    q_raw, fz = z[:, :MIX_WIDTH], z[:, MIX_WIDTH:2 * MIX_WIDTH]
    i_raw, g_raw = z[:, 2 * MIX_WIDTH:3 * MIX_WIDTH], z[:, 3 * MIX_WIDTH:]
    log_f = _log_sigmoid(fz) + jnp.log1p(lb * jnp.exp(jnp.minimum(-fz, EXP_CLIP)))
    k = (1.0 - lb) * _sigmoid(-fz)
    q = q_raw * _sigmoid(q_raw)
    o = _gla_block(q, k, i_raw, log_f, st_ref, pm_ref, ones, chunk=GLA_CHUNK, dk=HEAD_DIM, dv=HEAD_DIM)
    o_ref[...] = _head_rmsnorm(o, ones, nw_ref[...]) * _sigmoid(g_raw)


def _gla_kernel(z_ref, aw2_ref, ab_ref, nw_ref, pm_ref, qk_ones_ref, ones_ref, o_ref, st_ref):
    @pl.when(pl.program_id(1) == 0)
    def _():
        st_ref[...] = jnp.zeros_like(st_ref)

    z = z_ref[...]
    q = z[:, :GLA_QK] * (GLA_DK ** -0.5)
    k = z[:, GLA_QK:2 * GLA_QK]
    v = z[:, 2 * GLA_QK:2 * GLA_QK + MIX_WIDTH]
    gate = z[:, 2 * GLA_QK + MIX_WIDTH:2 * GLA_QK + 2 * MIX_WIDTH]
    a_low = z[:, 2 * GLA_QK + 2 * MIX_WIDTH:]
    log_alpha = _log_sigmoid(_mm(a_low, aw2_ref[...]) + ab_ref[...]) * (1.0 / GLA_TAU)
    o = _gla_block(q, k, v, log_alpha, st_ref, pm_ref, qk_ones_ref[...], chunk=GLA_CHUNK, dk=GLA_DK, dv=HEAD_DIM)
    o_ref[...] = _head_rmsnorm(o, ones_ref[...], nw_ref[...]) * (gate * _sigmoid(gate))


def _time_grid(T, seq):
    ts = TIME_BLOCK
    n_t = seq // ts
    return ts, n_t, (T // seq, n_t)


def _hgrn2(z_hg, lb_table, norm_w, layer, seq):
    T = z_hg.shape[0]
    ts, n_t, grid = _time_grid(T, seq)
    blk = lambda w: pl.BlockSpec((ts, w), lambda b, i: (b * n_t + i, 0))
    pm = _prefix_matrix(GLA_CHUNK)
    ones = _block_ones(N_HEADS, HEAD_DIM, HEAD_DIM)
    return pl.pallas_call(
        functools.partial(_hgrn2_kernel, layer=layer),
        grid=grid,
        in_specs=[blk(HG_WIDTH), _const_spec(lb_table.shape), _const_spec((1, MIX_WIDTH)),
                  _const_spec(pm.shape), _const_spec(ones.shape)],
        out_specs=blk(MIX_WIDTH),
        out_shape=jax.ShapeDtypeStruct((T, MIX_WIDTH), F32),
        scratch_shapes=[pltpu.VMEM((N_HEADS, HEAD_DIM, HEAD_DIM), F32)],
        compiler_params=_params(2),
        name="hgrn2",
    )(z_hg, lb_table, norm_w, pm, ones)


def _gla(z_gla, a_w2, a_b, norm_w, seq):
    T = z_gla.shape[0]
    ts, n_t, grid = _time_grid(T, seq)
    blk = lambda w: pl.BlockSpec((ts, w), lambda b, i: (b * n_t + i, 0))
    pm = _prefix_matrix(GLA_CHUNK)
    qk_ones = _block_ones(N_HEADS, GLA_DK, HEAD_DIM)
    ones = _block_ones(N_HEADS, HEAD_DIM, HEAD_DIM)
    return pl.pallas_call(
        _gla_kernel,
        grid=grid,
        in_specs=[blk(GLA_WIDTH), _const_spec(a_w2.shape), _const_spec((1, GLA_QK)), _const_spec((1, MIX_WIDTH)),
                  _const_spec(pm.shape), _const_spec(qk_ones.shape), _const_spec(ones.shape)],
        out_specs=blk(MIX_WIDTH),
        out_shape=jax.ShapeDtypeStruct((T, MIX_WIDTH), F32),
        scratch_shapes=[pltpu.VMEM((N_HEADS, HEAD_DIM, GLA_DK), F32)],
        compiler_params=_params(2),
        name="gla",
    )(z_gla, a_w2, a_b, norm_w, pm, qk_ones, ones)


_RW_R, _RW_K, _RW_V = 0, MIX_WIDTH, 2 * MIX_WIDTH
_RW_WL = 3 * MIX_WIDTH
_RW_AL = _RW_WL + RW_DECAY_LORA
_RW_GL = _RW_AL + RW_AAA_LORA


def _rwkv_kernel(*refs, has_vfirst):
    if has_vfirst:
        (z_ref, vf_ref, mu_ref, w0_ref, w2_ref, a0_ref, a2_ref, g2_ref, kk_ref, ka_ref, rk_ref, lnw_ref, lnb_ref,
         v0_ref, v1_ref, v2_ref, tri_ref, ones_ref, o_ref, s_ref, carry_ref) = refs
    else:
        (z_ref, mu_ref, w0_ref, w2_ref, a0_ref, a2_ref, g2_ref, kk_ref, ka_ref, rk_ref, lnw_ref, lnb_ref,
         tri_ref, ones_ref, o_ref, vout_ref, s_ref, carry_ref) = refs
    C, HD = RW_CHUNK, HEAD_DIM
    ts = z_ref.shape[0]
    nc = ts // C
    pairs = [(c, h) for c in range(nc) for h in range(N_HEADS)]

    @pl.when(pl.program_id(1) == 0)
    def _():
        s_ref[...] = jnp.zeros_like(s_ref)
        carry_ref[...] = jnp.zeros_like(carry_ref)

    z = z_ref[...]
    zp = pltpu.roll(z, 1, 0)
    zp = jnp.where(lax.broadcasted_iota(jnp.int32, z.shape, 0) == 0, carry_ref[...], zp)
    carry_ref[...] = z[ts - 1:ts, :]
    zs = z + mu_ref[...] * (zp - z)
    r, k, v = zs[:, _RW_R:_RW_K], zs[:, _RW_K:_RW_V], zs[:, _RW_V:_RW_WL]
    wl, al, gl = zs[:, _RW_WL:_RW_AL], zs[:, _RW_AL:_RW_GL], zs[:, _RW_GL:]
    ones = ones_ref[...]
    gsum = lambda t: _mm_sel_rhs(t, ones)
    w = -_softplus(-(w0_ref[...] + _mm(jnp.tanh(wl), w2_ref[...]))) - 0.5
    lw = -jnp.exp(w)
    a = _sigmoid(a0_ref[...] + _mm(al, a2_ref[...]))
    g = _mm(_sigmoid(gl), g2_ref[...])
    if has_vfirst:
        v = v + (vf_ref[...] - v) * _sigmoid(v0_ref[...] + _mm(_mm(v, v1_ref[...]), v2_ref[...]))
    else:
        vout_ref[...] = v
    kk = k * kk_ref[...]
    kk = kk / jnp.maximum(jnp.sqrt(gsum(kk * kk)), 1e-12)
    k_mod = k * (1.0 + (a - 1.0) * ka_ref[...])
    kka = kk * a

    tri = tri_ref[...]
    cums = [_mm_sel_lhs(tri, lw[c * C:(c + 1) * C]) for c in range(nc)]
    cum = jnp.concatenate(cums, axis=0)
    cum_last = jnp.concatenate([jnp.broadcast_to(cc[C - 1:C, :], (C, MIX_WIDTH)) for cc in cums], axis=0)
    w_end = [jnp.exp(cc[C - 1:C, :]) for cc in cums]
    r_t = r * jnp.exp(cum)
    a_t = -kk * jnp.exp(cum - lw)
    inv = jnp.exp(-cum)
    end = jnp.exp(cum_last - cum)
    r_tb, a_tb = r_t.astype(BF16), a_t.astype(BF16)
    k_hb, b_hb = (k_mod * inv).astype(BF16), (kka * inv).astype(BF16)
    k_eb, b_eb = (k_mod * end).astype(BF16), (kka * end).astype(BF16)
    vb = v.astype(BF16)
    blk = lambda t, c, h: t[c * C:(c + 1) * C, h * HD:(h + 1) * HD]

    ti = lax.broadcasted_iota(jnp.int32, (C, C), 0)
    si = lax.broadcasted_iota(jnp.int32, (C, C), 1)
    incl, strict = si <= ti, si < ti
    dot = lambda x, y: jnp.dot(x, y, preferred_element_type=F32)
    aa = {p: _mm_nt(jnp.concatenate([blk(r_tb, *p), blk(a_tb, *p)], axis=0),
                    jnp.concatenate([blk(k_hb, *p), blk(b_hb, *p)], axis=0)) for p in pairs}
    a_r = {p: jnp.concatenate([jnp.where(incl, aa[p][:C, :C], 0.0), jnp.where(incl, aa[p][:C, C:], 0.0)],
                              axis=1).astype(BF16) for p in pairs}
    a_ak = {p: jnp.where(strict, aa[p][C:, :C], 0.0).astype(BF16) for p in pairs}
    npow = {p: jnp.where(strict, aa[p][C:, C:], 0.0) for p in pairs}
    pv = {p: dot(a_ak[p], blk(vb, *p)) for p in pairs}
    x = {p: jnp.concatenate([blk(a_t, *p), pv[p]], axis=1) for p in pairs}
    n_stage = _n_levels(C)
    for j in range(n_stage):
        last = j == n_stage - 1
        prod = {}
        for p in pairs:
            nb, xb = npow[p].astype(BF16), x[p].astype(BF16)
            prod[p] = dot(nb, xb if last else jnp.concatenate([nb, xb], axis=1))
        for p in pairs:
            if last:
                x[p] = x[p] + prod[p]
            else:
                npow[p] = prod[p][:, :C]
                x[p] = x[p] + prod[p][:, C:]
    rmat = {p: jnp.concatenate([jnp.concatenate([jnp.zeros((C, HD), BF16), blk(vb, *p)], axis=1), x[p].astype(BF16)],
                               axis=0) for p in pairs}
    qz = {p: dot(a_r[p], rmat[p]) for p in pairs}
    gh = {p: lax.dot_general(rmat[p], jnp.concatenate([blk(k_eb, *p), blk(b_eb, *p)], axis=0),
                             (((0,), (0,)), ((), ())), preferred_element_type=F32) for p in pairs}
    q_f = {p: (blk(r_t, *p) + qz[p][:, :HD]).astype(BF16) for p in pairs}

    s = [s_ref[h] for h in range(N_HEADS)]
    y_rows = []
    for c in range(nc):
        sb = [s[h].astype(BF16) for h in range(N_HEADS)]
        ys = [lax.dot_general(q_f[c, h], sb[h], (((1,), (1,)), ((), ())), preferred_element_type=F32)
              + qz[c, h][:, HD:] for h in range(N_HEADS)]
        sg = [dot(sb[h], gh[c, h][:HD].astype(BF16)) for h in range(N_HEADS)]
        s = [s[h] * w_end[c][:, h * HD:(h + 1) * HD] + sg[h] + gh[c, h][HD:] for h in range(N_HEADS)]
        y_rows.append(jnp.concatenate(ys, axis=1))
    for h in range(N_HEADS):
        s_ref[h] = s[h]
    y = jnp.concatenate(y_rows, axis=0)

    mean = gsum(y) * (1.0 / HD)
    yc = y - mean
    var = gsum(yc * yc) * (1.0 / HD)
    yn = yc * lax.rsqrt(var + RW_GN_EPS) * lnw_ref[...] + lnb_ref[...]
    bonus = gsum(r * k_mod * rk_ref[...]) * v
    o_ref[...] = (yn + bonus) * g


def _rwkv(z_rw, v_first, p, seq):
    T = z_rw.shape[0]
    ts, n_t, grid = _time_grid(T, seq)
    blk = lambda w: pl.BlockSpec((ts, w), lambda b, i: (b * n_t + i, 0))
    tri = jnp.asarray(np.tril(np.ones((RW_CHUNK, RW_CHUNK), np.float32)), BF16)
    ones = _block_ones(N_HEADS, HEAD_DIM, HEAD_DIM)
    has_vfirst = v_first is not None
    names = ["mu", "w0", "w2", "a0", "a2", "g2", "k_k", "k_a", "r_k", "ln_w", "ln_b"]
    args = [z_rw] + ([v_first] if has_vfirst else []) + [p[n] for n in names]
    specs = [blk(RW_WIDTH)] + ([blk(MIX_WIDTH)] if has_vfirst else []) + [_const_spec(p[n].shape) for n in names]
    if has_vfirst:
        args += [p["v0"], p["v1"], p["v2"]]
        specs += [_const_spec(p[n].shape) for n in ("v0", "v1", "v2")]
    args += [tri, ones]
    specs += [_const_spec(tri.shape), _const_spec(ones.shape)]
    out_sds = jax.ShapeDtypeStruct((T, MIX_WIDTH), F32)
    res = pl.pallas_call(
        functools.partial(_rwkv_kernel, has_vfirst=has_vfirst),
        grid=grid,
        in_specs=specs,
        out_specs=blk(MIX_WIDTH) if has_vfirst else [blk(MIX_WIDTH), blk(MIX_WIDTH)],
        out_shape=out_sds if has_vfirst else [out_sds, out_sds],
        scratch_shapes=[pltpu.VMEM((N_HEADS, HEAD_DIM, HEAD_DIM), F32), pltpu.VMEM((1, RW_WIDTH), F32)],
        compiler_params=_params(2),
        name="rwkv7",
    )(*args)
    return (res, None) if has_vfirst else (res[0], res[1])


def _attn_kernel(q_ref, kp_ref, kc_ref, vp_ref, vc_ref, o_ref, lse_ref):
    qb = ATT_BLOCK
    i = pl.program_id(2)
    q = q_ref[0] * (HEAD_DIM ** -0.5)
    kp, kc, vp, vc = kp_ref[0], kc_ref[0], vp_ref[0], vc_ref[0]
    row = lax.broadcasted_iota(jnp.int32, (qb, qb), 0)
    col = lax.broadcasted_iota(jnp.int32, (qb, qb), 1)
    mask_prev = col >= row + jnp.where(i > 0, 0, qb)
    mask_cur = col <= row
    for h in range(N_HEADS):
        hs = slice(h * HEAD_DIM, (h + 1) * HEAD_DIM)
        s_p = jnp.where(mask_prev, _mm_nt(q[:, hs], kp[:, hs]), MASK_VALUE)
        s_c = jnp.where(mask_cur, _mm_nt(q[:, hs], kc[:, hs]), MASK_VALUE)
        m = jnp.maximum(jnp.max(s_p, axis=-1, keepdims=True), jnp.max(s_c, axis=-1, keepdims=True))
        p_p, p_c = jnp.exp(s_p - m), jnp.exp(s_c - m)
        l = jnp.sum(p_p, axis=-1, keepdims=True) + jnp.sum(p_c, axis=-1, keepdims=True)
        o_ref[0, :, hs] = (_mm(p_p, vp[:, hs]) + _mm(p_c, vc[:, hs])) / l
        lse_ref[0, :, hs] = jnp.broadcast_to(m + jnp.log(l), (qb, HEAD_DIM))


def _attention_group(z_at, g, dil, batch, seq):
    T = z_at.shape[0]
    n_sub = seq // dil
    nq = n_sub // ATT_BLOCK
    z3 = z_at.reshape(batch, n_sub, dil * 3 * AT_WIDTH)
    per_pos = 3 * AT_WIDTH // MIX_WIDTH

    def spec(col, prev):
        if prev:
            return pl.BlockSpec((1, ATT_BLOCK, MIX_WIDTH), lambda b, r, i: (b, jnp.maximum(i - 1, 0), r * per_pos + col))
        return pl.BlockSpec((1, ATT_BLOCK, MIX_WIDTH), lambda b, r, i: (b, i, r * per_pos + col))

    out_spec = pl.BlockSpec((1, ATT_BLOCK, MIX_WIDTH), lambda b, r, i: (b, i, r))
    sds = jax.ShapeDtypeStruct((batch, n_sub, dil * MIX_WIDTH), F32)
    o, lse = pl.pallas_call(
        _attn_kernel,
        grid=(batch, dil, nq),
        in_specs=[spec(g, False), spec(N_DIL + g, True), spec(N_DIL + g, False),
                  spec(2 * N_DIL + g, True), spec(2 * N_DIL + g, False)],
        out_specs=[out_spec, out_spec],
        out_shape=[sds, sds],
        compiler_params=_params(3),
        name=f"dilated_attn_{dil}",
    )(z3, z3, z3, z3, z3)
    return o.reshape(T, MIX_WIDTH), lse.reshape(T, MIX_WIDTH)


def _merge_kernel(x_ref, nw_ref, wg_ref, oa_ref, ob_ref, oc_ref, o0_ref, l0_ref, o1_ref, l1_ref, o2_ref, l2_ref,
                  wb_ref, wo_ref, out_ref):
    x = x_ref[...]
    hb = _rms(x, nw_ref[...]).astype(BF16)
    l0, l1, l2 = l0_ref[...], l1_ref[...], l2_ref[...]
    m = jnp.maximum(jnp.maximum(l0, l1), l2)
    e0, e1, e2 = jnp.exp(l0 - m), jnp.exp(l1 - m), jnp.exp(l2 - m)
    o_d = (e0 * o0_ref[...] + e1 * o1_ref[...] + e2 * o2_ref[...]) / (e0 + e1 + e2)
    acc = jnp.zeros(x.shape, F32)
    for kbr, o in enumerate((oa_ref[...], ob_ref[...], oc_ref[...], o_d)):
        gate = _sigmoid(jnp.dot(hb, wg_ref[:, kbr * D_MODEL:(kbr + 1) * D_MODEL], preferred_element_type=F32))
        acc = acc + gate * _mm(o, wb_ref[kbr])
    out_ref[...] = x + _mm(acc, wo_ref[...])


def _merge(x2, nw, w_gate, o_a, o_b, o_c, att, w_branch, w_out):
    T, D = x2.shape
    tm = ROW_TILE
    row = lambda w: pl.BlockSpec((tm, w), lambda i: (i, 0))
    return pl.pallas_call(
        _merge_kernel,
        grid=(T // tm,),
        in_specs=[row(D), _const_spec((1, D)), _const_spec(w_gate.shape)] + [row(MIX_WIDTH)] * 9
                 + [_const_spec(w_branch.shape), _const_spec(w_out.shape)],
        out_specs=row(D),
        out_shape=jax.ShapeDtypeStruct((T, D), F32),
        compiler_params=_params(1),
        name="merge",
    )(x2, nw, w_gate, o_a, o_b, o_c, *att, w_branch, w_out)


def _ffn_kernel(x_ref, nw_ref, wup_ref, cw_ref, cb_ref, wdn_ref, fw_ref, out_ref, tail_ref, *, n_s, final_norm):
    tm = x_ref.shape[0]

    @pl.when(pl.program_id(0) % n_s == 0)
    def _():
        tail_ref[...] = jnp.zeros_like(tail_ref)

    x = x_ref[...]
    hb = _rms(x, nw_ref[...]).astype(BF16)
    up = jnp.dot(hb, wup_ref[:, :D_FF], preferred_element_type=F32)
    gate = jnp.dot(hb, wup_ref[:, D_FF:], preferred_element_type=F32)
    row = lax.broadcasted_iota(jnp.int32, up.shape, 0)
    tail = tail_ref[...]
    up1 = jnp.where(row == 0, tail[1:2, :], pltpu.roll(up, 1, 0))
    up2 = jnp.where(row == 0, tail[0:1, :], jnp.where(row == 1, tail[1:2, :], pltpu.roll(up, 2, 0)))
    tail_ref[...] = up[tm - 2:tm, :]
    cw = cw_ref[...]
    conv = cb_ref[...] + cw[0:1, :] * up + cw[1:2, :] * up1 + cw[2:3, :] * up2
    act = conv * _sigmoid(conv) * gate
    y = x + _mm(act, wdn_ref[...])
    out_ref[...] = _rms(y, fw_ref[...]) if final_norm else y


def _ffn(x2, nw, w_up, conv_w, conv_b, w_down, final_w, seq, final_norm):
    T, D = x2.shape
    tm = ROW_TILE
    row = pl.BlockSpec((tm, D), lambda i: (i, 0))
    return pl.pallas_call(
        functools.partial(_ffn_kernel, n_s=seq // tm, final_norm=final_norm),
        grid=(T // tm,),
        in_specs=[row, _const_spec((1, D)), _const_spec(w_up.shape), _const_spec(conv_w.shape),
                  _const_spec((1, D_FF)), _const_spec(w_down.shape), _const_spec((1, D))],
        out_specs=row,
        out_shape=jax.ShapeDtypeStruct((T, D), F32),
        scratch_shapes=[pltpu.VMEM((CONV_W - 1, D_FF), F32)],
        compiler_params=_params(1),
        name="convglu",
    )(x2, nw, w_up, conv_w, conv_b, w_down, final_w)


def kernel(x, norm_mix_w, norm_ffn_w, norm_final_w, w_in, hg_lb_table, hg_norm_w, gla_a_w2, gla_a_b, gla_norm_w, rw_mu, rw_w0, rw_w2, rw_a0, rw_a2, rw_g2, rw_k_k, rw_k_a, rw_r_k, rw_ln_w, rw_ln_b, rw_v0, rw_v1, rw_v2, w_branch, w_out, ffn_w_up, ffn_conv_w, ffn_conv_b, ffn_w_down):
    B, S, D = x.shape
    T = B * S
    x2 = x.reshape(T, D)
    cos_t, sin_t = _rope_tables(S)
    row = lambda t: t.reshape(1, -1)
    W = MIX_WIDTH
    src = np.concatenate([np.arange(0, W), np.arange(W + RW_DECAY_LORA, 3 * W + RW_DECAY_LORA),
                          np.arange(W, W + RW_DECAY_LORA), np.arange(3 * W + RW_DECAY_LORA, RW_WIDTH)])
    o_hg, o_gla, o_rw, o_at = 0, HG_WIDTH, HG_WIDTH + GLA_WIDTH, HG_WIDTH + GLA_WIDTH + RW_WIDTH
    o_gate = o_at + 3 * AT_WIDTH
    v_first = None
    for layer in range(DEPTH):
        w = w_in[layer]
        w_hg = w[:, o_hg:o_gla].astype(BF16)
        w_gla = w[:, o_gla:o_rw].astype(BF16)
        w_rw = w[:, o_rw:o_at][:, src].astype(BF16)
        w_at = w[:, o_at:o_gate].astype(BF16)
        w_gate = w[:, o_gate:].astype(BF16)
        z_hg, z_gla, z_rw, z_at = _inproj(x2, row(norm_mix_w[layer]), cos_t, sin_t, w_hg, w_gla, w_rw, w_at, S)

        o_a = _hgrn2(z_hg, hg_lb_table, row(hg_norm_w[layer]), layer, S)
        o_b = _gla(z_gla, gla_a_w2[layer].astype(BF16), row(gla_a_b[layer]), row(gla_norm_w[layer]), S)
        p = {"mu": row(rw_mu[layer][src]), "w0": row(rw_w0[layer]), "w2": rw_w2[layer].astype(BF16),
             "a0": row(rw_a0[layer]), "a2": rw_a2[layer].astype(BF16), "g2": rw_g2[layer].astype(BF16),
             "k_k": row(rw_k_k[layer]), "k_a": row(rw_k_a[layer]), "r_k": row(rw_r_k[layer]),
             "ln_w": row(rw_ln_w[layer]), "ln_b": row(rw_ln_b[layer])}
        if layer > 0:
            p.update(v0=row(rw_v0[layer - 1]), v1=rw_v1[layer - 1].astype(BF16), v2=rw_v2[layer - 1].astype(BF16))
        o_c, v_new = _rwkv(z_rw, v_first, p, S)
        if layer == 0:
            v_first = v_new
        att = []
        for g, (_, dil) in enumerate(DIL_PAIRS):
            att.extend(_attention_group(z_at, g, dil, B, S))

        x2 = _merge(x2, row(norm_mix_w[layer]), w_gate, o_a, o_b, o_c, att,
                    w_branch[layer].astype(BF16), w_out[layer].astype(BF16))
        x2 = _ffn(x2, row(norm_ffn_w[layer]), ffn_w_up[layer].astype(BF16), ffn_conv_w[layer],
                  row(ffn_conv_b[layer]), ffn_w_down[layer].astype(BF16), row(norm_final_w), S,
                  final_norm=(layer == DEPTH - 1))
    return x2.reshape(B, S, D)
```

```python
import functools

import numpy as np
import jax
import jax.numpy as jnp
from jax import lax
from jax.experimental import pallas as pl
from jax.experimental.pallas import tpu as pltpu

F32 = jnp.float32
BF16 = jnp.bfloat16

D_MODEL = 1024
DEPTH = 2
HEAD_DIM = 64
MIX_WIDTH = D_MODEL // 4
N_BRANCH = 4
NORM_EPS = 1e-6
MASK_VALUE = -1e9
EXP_CLIP = 60.0
N_HEADS = MIX_WIDTH // HEAD_DIM
GLA_DK = 32
GLA_RANK = 16
GLA_TAU = 16.0
RW_DECAY_LORA = 32
RW_AAA_LORA = 32
RW_MV_LORA = 32
RW_GATE_LORA = 64
RW_GN_EPS = 64e-5
DIL_PAIRS = ((128, 1), (512, 4), (2048, 16))
N_DIL = 3
AT_WIDTH = N_DIL * N_HEADS * HEAD_DIM
ROPE_THETA = 10000.0
D_FF = 11 * D_MODEL // 4
CONV_W = 3

HG_WIDTH = 4 * MIX_WIDTH
GLA_QK = N_HEADS * GLA_DK
GLA_WIDTH = 2 * GLA_QK + 2 * MIX_WIDTH + GLA_RANK
RW_WIDTH = 3 * MIX_WIDTH + RW_DECAY_LORA + RW_AAA_LORA + RW_GATE_LORA
GATE_WIDTH = N_BRANCH * D_MODEL

LANES = 128
VMEM_LIMIT_BYTES = 56 * 1024 * 1024
ROW_TILE = 256
TIME_BLOCK = 512
GLA_CHUNK = 128
RW_CHUNK = 64
ATT_BLOCK = 128


def _const_spec(shape):
    nd = len(shape)
    return pl.BlockSpec(shape, lambda *_: (0,) * nd, pipeline_mode=pl.Buffered(1))


def _params(n_grid):
    return pltpu.CompilerParams(dimension_semantics=("arbitrary",) * n_grid, vmem_limit_bytes=VMEM_LIMIT_BYTES)


def _mm(a, b):
    return jnp.dot(a.astype(BF16), b.astype(BF16), preferred_element_type=F32)


def _mm_nt(a, b):
    return lax.dot_general(a.astype(BF16), b.astype(BF16), (((1,), (1,)), ((), ())), preferred_element_type=F32)


def _mm_tn(a, b):
    return lax.dot_general(a.astype(BF16), b.astype(BF16), (((0,), (0,)), ((), ())), preferred_element_type=F32)


def _split2(a):
    hi = a.astype(BF16)
    lo = (a - hi.astype(F32)).astype(BF16)
    return hi, lo


def _mm_sel_lhs(sel, a):
    hi, lo = _split2(a)
    return jnp.dot(sel, hi, preferred_element_type=F32) + jnp.dot(sel, lo, preferred_element_type=F32)


def _mm_sel_rhs(a, sel):
    hi, lo = _split2(a)
    return jnp.dot(hi, sel, preferred_element_type=F32) + jnp.dot(lo, sel, preferred_element_type=F32)


def _rms(x, w):
    return x * lax.rsqrt(jnp.mean(x * x, axis=-1, keepdims=True) + NORM_EPS) * w


def _sigmoid(x):
    return jax.nn.sigmoid(x)


def _log_sigmoid(x):
    return jnp.minimum(x, 0.0) - jnp.log1p(jnp.exp(-jnp.abs(x)))


def _softplus(x):
    return jnp.maximum(x, 0.0) + jnp.log1p(jnp.exp(-jnp.abs(x)))


def _n_levels(chunk):
    return int(np.log2(chunk))


def _prefix_matrix(chunk):
    C, L = chunk, _n_levels(chunk)
    M = np.zeros(((L + 1) * C, C), np.float32)
    M[:C] = np.tril(np.ones((C, C), np.float32))
    for l in range(L):
        m = C >> (l + 1)
        for t in range(C):
            mid = (t // (2 * m)) * 2 * m + m - 1
            if t % (2 * m) >= m:
                M[(l + 1) * C + t, mid + 1:t + 1] = 1.0
            else:
                M[(l + 1) * C + t, t + 1:mid + 1] = 1.0
    return jnp.asarray(M, BF16)


def _block_ones(n_groups, w_in, w_out):
    return jnp.asarray(np.kron(np.eye(n_groups, dtype=np.float32), np.ones((w_in, w_out), np.float32)), BF16)


def _rope_tables(seq):
    half = HEAD_DIM // 2
    inv = ROPE_THETA ** (-jnp.arange(half, dtype=F32) / half)
    ang = jnp.arange(seq, dtype=jnp.int32).astype(F32)[:, None] * inv[None, :]
    cos, sin = jnp.cos(ang), jnp.sin(ang)
    cos_h = jnp.concatenate([cos, cos], axis=1)
    sin_h = jnp.concatenate([-sin, sin], axis=1)
    return jnp.tile(cos_h, (1, N_HEADS)), jnp.tile(sin_h, (1, N_HEADS))


QKV_WIDTH = 3 * MIX_WIDTH


def _inproj_kernel(x_ref, nw_ref, cos_ref, sin_ref, whg_ref, wgla_ref, wrw_ref, wat_ref,
                   zhg_ref, zgla_ref, zrw_ref, at1_ref, at4_ref, at16_ref, at_scr):
    tm = x_ref.shape[0]
    hb = _rms(x_ref[...], nw_ref[...]).astype(BF16)
    zhg_ref[...] = jnp.dot(hb, whg_ref[...], preferred_element_type=F32)
    zgla_ref[...] = jnp.dot(hb, wgla_ref[...], preferred_element_type=F32)
    zrw_ref[...] = jnp.dot(hb, wrw_ref[...], preferred_element_type=F32)
    zat = jnp.dot(hb, wat_ref[...], preferred_element_type=F32)
    cos, sin = cos_ref[...], sin_ref[...]
    lane = lax.broadcasted_iota(jnp.int32, cos.shape, 1)
    first_half = (lane % HEAD_DIM) < (HEAD_DIM // 2)

    def rope(p):
        partner = jnp.where(first_half, pltpu.roll(p, MIX_WIDTH - HEAD_DIM // 2, 1), pltpu.roll(p, HEAD_DIM // 2, 1))
        return p * cos + partner * sin

    n_lane_blk = QKV_WIDTH // LANES
    for g, (out_ref, (_, dil)) in enumerate(zip((at1_ref, at4_ref, at16_ref), DIL_PAIRS)):
        c0 = g * QKV_WIDTH
        q = rope(zat[:, c0:c0 + MIX_WIDTH]) * (HEAD_DIM ** -0.5)
        k = rope(zat[:, c0 + MIX_WIDTH:c0 + 2 * MIX_WIDTH])
        qkv = jnp.concatenate([q, k, zat[:, c0 + 2 * MIX_WIDTH:c0 + QKV_WIDTH]], axis=1)
        if dil == 1:
            out_ref[...] = qkv.astype(BF16)
            continue
        for c in range(n_lane_blk):
            at_scr[c] = qkv[:, c * LANES:(c + 1) * LANES]
        for r in range(dil):
            rows = jnp.concatenate([at_scr[c, pl.ds(r, tm // dil, stride=dil), :] for c in range(n_lane_blk)], axis=1)
            out_ref[:, r * QKV_WIDTH:(r + 1) * QKV_WIDTH] = rows.astype(BF16)


def _inproj(x2, nw, cos_t, sin_t, w_hg, w_gla, w_rw, w_at, seq):
    T, D = x2.shape
    tm = ROW_TILE
    n_s = seq // tm
    row = lambda w: pl.BlockSpec((tm, w), lambda i: (i, 0))
    pos = pl.BlockSpec((tm, MIX_WIDTH), lambda i: (i % n_s, 0))
    widths = (HG_WIDTH, GLA_WIDTH, RW_WIDTH)
    dils = [d for _, d in DIL_PAIRS]
    return pl.pallas_call(
        _inproj_kernel,
        grid=(T // tm,),
        in_specs=[row(D), _const_spec((1, D)), pos, pos,
                  _const_spec(w_hg.shape), _const_spec(w_gla.shape), _const_spec(w_rw.shape), _const_spec(w_at.shape)],
        out_specs=[row(w) for w in widths]
                  + [pl.BlockSpec((tm // d, d * QKV_WIDTH), lambda i: (i, 0)) for d in dils],
        out_shape=[jax.ShapeDtypeStruct((T, w), F32) for w in widths]
                  + [jax.ShapeDtypeStruct((T // d, d * QKV_WIDTH), BF16) for d in dils],
        scratch_shapes=[pltpu.VMEM((QKV_WIDTH // LANES, tm, LANES), F32)],
        compiler_params=_params(1),
        name="inproj",
    )(x2, nw, cos_t, sin_t, w_hg, w_gla, w_rw, w_at)


def _level_masks(chunk):
    t = lax.broadcasted_iota(jnp.int32, (chunk, chunk), 0)
    s = lax.broadcasted_iota(jnp.int32, (chunk, chunk), 1)
    masks = []
    for l in range(_n_levels(chunk)):
        m = chunk >> (l + 1)
        sh = int(np.log2(2 * m))
        masks.append(((t >> sh) == (s >> sh)) & ((t & m) != 0) & ((s & m) == 0))
    return masks


def _gla_block(q, k, v, g, st_ref, pm_ref, qk_ones, *, chunk, dk, dv):
    C = chunk
    nc = q.shape[0] // C
    n_lvl = _n_levels(C)
    masks = _level_masks(C)
    pm = pm_ref[...]
    rows = lambda t, c: t[c * C:(c + 1) * C]
    stack = lambda f: jnp.concatenate([f(c) for c in range(nc)], axis=0)
    D = [_mm_sel_lhs(pm, rows(g, c)) for c in range(nc)]
    b = stack(lambda c: D[c][:C])
    b_last = stack(lambda c: jnp.broadcast_to(D[c][C - 1:C, :], (C, b.shape[1])))
    dec_last = [jnp.exp(D[c][C - 1:C, :]) for c in range(nc)]
    q_in = (q * jnp.exp(b)).astype(BF16)
    k_dec = (k * jnp.exp(b_last - b)).astype(BF16)
    vb = v.astype(BF16)
    qf, kf = [], []
    for l in range(n_lvl):
        e = jnp.exp(stack(lambda c: D[c][(l + 1) * C:(l + 2) * C]))
        qf.append((q * e).astype(BF16))
        kf.append((k * e).astype(BF16))
    diag = _mm(q * k, qk_ones)
    pairs = [(c, h) for c in range(nc) for h in range(N_HEADS)]
    bk = lambda t, c, h: t[c * C:(c + 1) * C, h * dk:(h + 1) * dk]
    bv = lambda t, c, h: t[c * C:(c + 1) * C, h * dv:(h + 1) * dv]
    nt = (((1,), (1,)), ((), ()))
    tn = (((0,), (0,)), ((), ()))
    prod = {(p, l): lax.dot_general(bk(qf[l], *p), bk(kf[l], *p), nt, preferred_element_type=F32)
            for p in pairs for l in range(n_lvl)}
    o_intra, kv = {}, {}
    for p in pairs:
        sc = jnp.zeros((C, C), F32)
        for l in range(n_lvl):
            sc = jnp.where(masks[l], prod[p, l], sc)
        o_intra[p] = jnp.dot(sc.astype(BF16), bv(vb, *p), preferred_element_type=F32)
        kv[p] = lax.dot_general(bv(vb, *p), bk(k_dec, *p), tn, preferred_element_type=F32)
    s = [st_ref[h] for h in range(N_HEADS)]
    out_rows = []
    for c in range(nc):
        o_inter = [lax.dot_general(bk(q_in, c, h), s[h].astype(BF16), nt, preferred_element_type=F32)
                   for h in range(N_HEADS)]
        s = [s[h] * dec_last[c][:, h * dk:(h + 1) * dk] + kv[c, h] for h in range(N_HEADS)]
        out_rows.append(jnp.concatenate([o_intra[c, h] + o_inter[h] for h in range(N_HEADS)], axis=1))
    for h in range(N_HEADS):
        st_ref[h] = s[h]
    return jnp.concatenate(out_rows, axis=0) + diag * v


def _head_rmsnorm(o, head_ones, w):
    ms = _mm_sel_rhs(o * o, head_ones) * (1.0 / HEAD_DIM)
    return o * lax.rsqrt(ms + NORM_EPS) * w


def _hgrn2_kernel(z_ref, lbt_ref, nw_ref, pm_ref, ones_ref, o_ref, st_ref, *, layer):
    @pl.when(pl.program_id(1) == 0)
    def _():
        st_ref[...] = jnp.zeros_like(st_ref)

    tab = lbt_ref[...]
    e = jnp.exp(tab - jnp.max(tab, axis=0, keepdims=True))
    p = e / jnp.sum(e, axis=0, keepdims=True)
    lb = jnp.zeros((1, MIX_WIDTH), F32)
    for i in range(1, layer + 1):
        lb = lb + p[i:i + 1, :]
    ones = ones_ref[...]
    z = z_ref[...]
    q_raw, fz = z[:, :MIX_WIDTH], z[:, MIX_WIDTH:2 * MIX_WIDTH]
    i_raw, g_raw = z[:, 2 * MIX_WIDTH:3 * MIX_WIDTH], z[:, 3 * MIX_WIDTH:]
    log_f = _log_sigmoid(fz) + jnp.log1p(lb * jnp.exp(jnp.minimum(-fz, EXP_CLIP)))
    k = (1.0 - lb) * _sigmoid(-fz)
    q = q_raw * _sigmoid(q_raw)
    o = _gla_block(q, k, i_raw, log_f, st_ref, pm_ref, ones, chunk=GLA_CHUNK, dk=HEAD_DIM, dv=HEAD_DIM)
    o_ref[...] = _head_rmsnorm(o, ones, nw_ref[...]) * _sigmoid(g_raw)


def _gla_kernel(z_ref, aw2_ref, ab_ref, nw_ref, pm_ref, qk_ones_ref, ones_ref, o_ref, st_ref):
    @pl.when(pl.program_id(1) == 0)
    def _():
        st_ref[...] = jnp.zeros_like(st_ref)

    z = z_ref[...]
    q = z[:, :GLA_QK] * (GLA_DK ** -0.5)
    k = z[:, GLA_QK:2 * GLA_QK]
    v = z[:, 2 * GLA_QK:2 * GLA_QK + MIX_WIDTH]
    gate = z[:, 2 * GLA_QK + MIX_WIDTH:2 * GLA_QK + 2 * MIX_WIDTH]
    a_low = z[:, 2 * GLA_QK + 2 * MIX_WIDTH:]
    log_alpha = _log_sigmoid(_mm(a_low, aw2_ref[...]) + ab_ref[...]) * (1.0 / GLA_TAU)
    o = _gla_block(q, k, v, log_alpha, st_ref, pm_ref, qk_ones_ref[...], chunk=GLA_CHUNK, dk=GLA_DK, dv=HEAD_DIM)
    o_ref[...] = _head_rmsnorm(o, ones_ref[...], nw_ref[...]) * (gate * _sigmoid(gate))


def _time_grid(T, seq):
    ts = TIME_BLOCK
    n_t = seq // ts
    return ts, n_t, (T // seq, n_t)


def _hgrn2(z_hg, lb_table, norm_w, layer, seq):
    T = z_hg.shape[0]
    ts, n_t, grid = _time_grid(T, seq)
    blk = lambda w: pl.BlockSpec((ts, w), lambda b, i: (b * n_t + i, 0))
    pm = _prefix_matrix(GLA_CHUNK)
    ones = _block_ones(N_HEADS, HEAD_DIM, HEAD_DIM)
    return pl.pallas_call(
        functools.partial(_hgrn2_kernel, layer=layer),
        grid=grid,
        in_specs=[blk(HG_WIDTH), _const_spec(lb_table.shape), _const_spec((1, MIX_WIDTH)),
                  _const_spec(pm.shape), _const_spec(ones.shape)],
        out_specs=blk(MIX_WIDTH),
        out_shape=jax.ShapeDtypeStruct((T, MIX_WIDTH), F32),
        scratch_shapes=[pltpu.VMEM((N_HEADS, HEAD_DIM, HEAD_DIM), F32)],
        compiler_params=_params(2),
        name="hgrn2",
    )(z_hg, lb_table, norm_w, pm, ones)


def _gla(z_gla, a_w2, a_b, norm_w, seq):
    T = z_gla.shape[0]
    ts, n_t, grid = _time_grid(T, seq)
    blk = lambda w: pl.BlockSpec((ts, w), lambda b, i: (b * n_t + i, 0))
    pm = _prefix_matrix(GLA_CHUNK)
    qk_ones = _block_ones(N_HEADS, GLA_DK, HEAD_DIM)
    ones = _block_ones(N_HEADS, HEAD_DIM, HEAD_DIM)
    return pl.pallas_call(
        _gla_kernel,
        grid=grid,
        in_specs=[blk(GLA_WIDTH), _const_spec(a_w2.shape), _const_spec((1, GLA_QK)), _const_spec((1, MIX_WIDTH)),
                  _const_spec(pm.shape), _const_spec(qk_ones.shape), _const_spec(ones.shape)],
        out_specs=blk(MIX_WIDTH),
        out_shape=jax.ShapeDtypeStruct((T, MIX_WIDTH), F32),
        scratch_shapes=[pltpu.VMEM((N_HEADS, HEAD_DIM, GLA_DK), F32)],
        compiler_params=_params(2),
        name="gla",
    )(z_gla, a_w2, a_b, norm_w, pm, qk_ones, ones)


_RW_R, _RW_K, _RW_V = 0, MIX_WIDTH, 2 * MIX_WIDTH
_RW_WL = 3 * MIX_WIDTH
_RW_AL = _RW_WL + RW_DECAY_LORA
_RW_GL = _RW_AL + RW_AAA_LORA


def _rwkv_kernel(*refs, has_vfirst):
    if has_vfirst:
        (z_ref, vf_ref, mu_ref, w0_ref, w2_ref, a0_ref, a2_ref, g2_ref, kk_ref, ka_ref, rk_ref, lnw_ref, lnb_ref,
         v0_ref, v1_ref, v2_ref, tri_ref, ones_ref, o_ref, s_ref, carry_ref) = refs
    else:
        (z_ref, mu_ref, w0_ref, w2_ref, a0_ref, a2_ref, g2_ref, kk_ref, ka_ref, rk_ref, lnw_ref, lnb_ref,
         tri_ref, ones_ref, o_ref, vout_ref, s_ref, carry_ref) = refs
    C, HD = RW_CHUNK, HEAD_DIM
    ts = z_ref.shape[0]
    nc = ts // C
    pairs = [(c, h) for c in range(nc) for h in range(N_HEADS)]

    @pl.when(pl.program_id(1) == 0)
    def _():
        s_ref[...] = jnp.zeros_like(s_ref)
        carry_ref[...] = jnp.zeros_like(carry_ref)

    z = z_ref[...]
    zp = pltpu.roll(z, 1, 0)
    zp = jnp.where(lax.broadcasted_iota(jnp.int32, z.shape, 0) == 0, carry_ref[...], zp)
    carry_ref[...] = z[ts - 1:ts, :]
    zs = z + mu_ref[...] * (zp - z)
    r, k, v = zs[:, _RW_R:_RW_K], zs[:, _RW_K:_RW_V], zs[:, _RW_V:_RW_WL]
    wl, al, gl = zs[:, _RW_WL:_RW_AL], zs[:, _RW_AL:_RW_GL], zs[:, _RW_GL:]
    ones = ones_ref[...]
    gsum = lambda t: _mm_sel_rhs(t, ones)
    w = -_softplus(-(w0_ref[...] + _mm(jnp.tanh(wl), w2_ref[...]))) - 0.5
    lw = -jnp.exp(w)
    a = _sigmoid(a0_ref[...] + _mm(al, a2_ref[...]))
    g = _mm(_sigmoid(gl), g2_ref[...])
    if has_vfirst:
        v = v + (vf_ref[...] - v) * _sigmoid(v0_ref[...] + _mm(_mm(v, v1_ref[...]), v2_ref[...]))
    else:
        vout_ref[...] = v
    kk = k * kk_ref[...]
    kk = kk / jnp.maximum(jnp.sqrt(gsum(kk * kk)), 1e-12)
    k_mod = k * (1.0 + (a - 1.0) * ka_ref[...])
    kka = kk * a

    tri = tri_ref[...]
    cums = [_mm_sel_lhs(tri, lw[c * C:(c + 1) * C]) for c in range(nc)]
    cum = jnp.concatenate(cums, axis=0)
    cum_last = jnp.concatenate([jnp.broadcast_to(cc[C - 1:C, :], (C, MIX_WIDTH)) for cc in cums], axis=0)
    w_end = [jnp.exp(cc[C - 1:C, :]) for cc in cums]
    r_t = r * jnp.exp(cum)
    a_t = -kk * jnp.exp(cum - lw)
    inv = jnp.exp(-cum)
    end = jnp.exp(cum_last - cum)
    r_tb, a_tb = r_t.astype(BF16), a_t.astype(BF16)
    k_hb, b_hb = (k_mod * inv).astype(BF16), (kka * inv).astype(BF16)
    k_eb, b_eb = (k_mod * end).astype(BF16), (kka * end).astype(BF16)
    vb = v.astype(BF16)
    blk = lambda t, c, h: t[c * C:(c + 1) * C, h * HD:(h + 1) * HD]

    ti = lax.broadcasted_iota(jnp.int32, (C, C), 0)
    si = lax.broadcasted_iota(jnp.int32, (C, C), 1)
    incl, strict = si <= ti, si < ti
    dot = lambda x, y: jnp.dot(x, y, preferred_element_type=F32)
    aa = {p: _mm_nt(jnp.concatenate([blk(r_tb, *p), blk(a_tb, *p)], axis=0),
                    jnp.concatenate([blk(k_hb, *p), blk(b_hb, *p)], axis=0)) for p in pairs}
    a_r = {p: jnp.concatenate([jnp.where(incl, aa[p][:C, :C], 0.0), jnp.where(incl, aa[p][:C, C:], 0.0)],
                              axis=1).astype(BF16) for p in pairs}
    a_ak = {p: jnp.where(strict, aa[p][C:, :C], 0.0).astype(BF16) for p in pairs}
    npow = {p: jnp.where(strict, aa[p][C:, C:], 0.0) for p in pairs}
    pv = {p: dot(a_ak[p], blk(vb, *p)) for p in pairs}
    x = {p: jnp.concatenate([blk(a_t, *p), pv[p]], axis=1) for p in pairs}
    n_stage = _n_levels(C)
    for j in range(n_stage):
        last = j == n_stage - 1
        prod = {}
        for p in pairs:
            nb, xb = npow[p].astype(BF16), x[p].astype(BF16)
            prod[p] = dot(nb, xb if last else jnp.concatenate([nb, xb], axis=1))
        for p in pairs:
            if last:
                x[p] = x[p] + prod[p]
            else:
                npow[p] = prod[p][:, :C]
                x[p] = x[p] + prod[p][:, C:]
    rmat = {p: jnp.concatenate([jnp.concatenate([jnp.zeros((C, HD), BF16), blk(vb, *p)], axis=1), x[p].astype(BF16)],
                               axis=0) for p in pairs}
    qz = {p: dot(a_r[p], rmat[p]) for p in pairs}
    gh = {p: lax.dot_general(rmat[p], jnp.concatenate([blk(k_eb, *p), blk(b_eb, *p)], axis=0),
                             (((0,), (0,)), ((), ())), preferred_element_type=F32) for p in pairs}
    q_f = {p: (blk(r_t, *p) + qz[p][:, :HD]).astype(BF16) for p in pairs}

    s = [s_ref[h] for h in range(N_HEADS)]
    y_rows = []
    for c in range(nc):
        sb = [s[h].astype(BF16) for h in range(N_HEADS)]
        ys = [lax.dot_general(q_f[c, h], sb[h], (((1,), (1,)), ((), ())), preferred_element_type=F32)
              + qz[c, h][:, HD:] for h in range(N_HEADS)]
        sg = [dot(sb[h], gh[c, h][:HD].astype(BF16)) for h in range(N_HEADS)]
        s = [s[h] * w_end[c][:, h * HD:(h + 1) * HD] + sg[h] + gh[c, h][HD:] for h in range(N_HEADS)]
        y_rows.append(jnp.concatenate(ys, axis=1))
    for h in range(N_HEADS):
        s_ref[h] = s[h]
    y = jnp.concatenate(y_rows, axis=0)

    mean = gsum(y) * (1.0 / HD)
    yc = y - mean
    var = gsum(yc * yc) * (1.0 / HD)
    yn = yc * lax.rsqrt(var + RW_GN_EPS) * lnw_ref[...] + lnb_ref[...]
    bonus = gsum(r * k_mod * rk_ref[...]) * v
    o_ref[...] = (yn + bonus) * g


def _rwkv(z_rw, v_first, p, seq):
    T = z_rw.shape[0]
    ts, n_t, grid = _time_grid(T, seq)
    blk = lambda w: pl.BlockSpec((ts, w), lambda b, i: (b * n_t + i, 0))
    tri = jnp.asarray(np.tril(np.ones((RW_CHUNK, RW_CHUNK), np.float32)), BF16)
    ones = _block_ones(N_HEADS, HEAD_DIM, HEAD_DIM)
    has_vfirst = v_first is not None
    names = ["mu", "w0", "w2", "a0", "a2", "g2", "k_k", "k_a", "r_k", "ln_w", "ln_b"]
    args = [z_rw] + ([v_first] if has_vfirst else []) + [p[n] for n in names]
    specs = [blk(RW_WIDTH)] + ([blk(MIX_WIDTH)] if has_vfirst else []) + [_const_spec(p[n].shape) for n in names]
    if has_vfirst:
        args += [p["v0"], p["v1"], p["v2"]]
        specs += [_const_spec(p[n].shape) for n in ("v0", "v1", "v2")]
    args += [tri, ones]
    specs += [_const_spec(tri.shape), _const_spec(ones.shape)]
    out_sds = jax.ShapeDtypeStruct((T, MIX_WIDTH), F32)
    res = pl.pallas_call(
        functools.partial(_rwkv_kernel, has_vfirst=has_vfirst),
        grid=grid,
        in_specs=specs,
        out_specs=blk(MIX_WIDTH) if has_vfirst else [blk(MIX_WIDTH), blk(MIX_WIDTH)],
        out_shape=out_sds if has_vfirst else [out_sds, out_sds],
        scratch_shapes=[pltpu.VMEM((N_HEADS, HEAD_DIM, HEAD_DIM), F32), pltpu.VMEM((1, RW_WIDTH), F32)],
        compiler_params=_params(2),
        name="rwkv7",
    )(*args)
    return (res, None) if has_vfirst else (res[0], res[1])


ATT_TOKENS = ATT_BLOCK * max(d for _, d in DIL_PAIRS)
ATT_UNITS_PER_BATCH = 8


def _attn_kernel(c1_ref, p1_ref, c4_ref, p4_ref, c16_ref, p16_ref, o_ref, o_scr, l_scr):
    qb = ATT_BLOCK
    row = lax.broadcasted_iota(jnp.int32, (qb, 2 * qb), 0)
    col = lax.broadcasted_iota(jnp.int32, (qb, 2 * qb), 1)
    band = (col >= row) & (col <= row + qb)
    band_first = band & (col >= jnp.where(pl.program_id(1) > 0, 0, qb))
    nt = (((1,), (1,)), ((), ()))
    curs, prevs = (c1_ref, c4_ref, c16_ref), (p1_ref, p4_ref, p16_ref)
    units = [(g, dil, r, qi) for g, (_, dil) in enumerate(DIL_PAIRS) for r in range(dil)
             for qi in range(ATT_TOKENS // dil // qb)]
    heads = range(N_HEADS)
    hsl = lambda t, h: t[:, h * HEAD_DIM:(h + 1) * HEAD_DIM]
    for b0 in range(0, len(units), ATT_UNITS_PER_BATCH):
        batch = units[b0:b0 + ATT_UNITS_PER_BATCH]
        vs, masks, scores = [], [], []
        for g, dil, r, qi in batch:
            cur, prev = curs[g], prevs[g]
            qc, kc, vc = (r * QKV_WIDTH + j * MIX_WIDTH for j in range(3))
            q = cur[0, qi * qb:(qi + 1) * qb, qc:qc + MIX_WIDTH]
            if qi == 0:
                k = jnp.concatenate([prev[0, :, kc:kc + MIX_WIDTH], cur[0, :qb, kc:kc + MIX_WIDTH]], axis=0)
                v = jnp.concatenate([prev[0, :, vc:vc + MIX_WIDTH], cur[0, :qb, vc:vc + MIX_WIDTH]], axis=0)
            else:
                k = cur[0, (qi - 1) * qb:(qi + 1) * qb, kc:kc + MIX_WIDTH]
                v = cur[0, (qi - 1) * qb:(qi + 1) * qb, vc:vc + MIX_WIDTH]
            vs.append(v)
            masks.append(band_first if qi == 0 else band)
            scores.append([lax.dot_general(hsl(q, h), hsl(k, h), nt, preferred_element_type=F32) for h in heads])
        probs, stats = [], []
        for u in range(len(batch)):
            pu, su = [], []
            for h in heads:
                s = jnp.where(masks[u], scores[u][h], MASK_VALUE)
                m = jnp.max(s, axis=-1, keepdims=True)
                p = jnp.exp(s - m)
                su.append((m, jnp.sum(p, axis=-1, keepdims=True)))
                pu.append(p.astype(BF16))
            probs.append(pu)
            stats.append(su)
        pvs = [[jnp.dot(probs[u][h], hsl(vs[u], h), preferred_element_type=F32) for h in heads]
               for u in range(len(batch))]
        for u, (g, dil, r, qi) in enumerate(batch):
            outs = [pvs[u][h] / stats[u][h][1] for h in heads]
            lses = [jnp.broadcast_to(stats[u][h][0] + jnp.log(stats[u][h][1]), (qb, HEAD_DIM)) for h in heads]
            tok = pl.ds(qi * qb * dil + r, qb, stride=dil) if dil > 1 else pl.ds(qi * qb, qb)
            for c in range(MIX_WIDTH // LANES):
                hh = slice(c * LANES // HEAD_DIM, (c + 1) * LANES // HEAD_DIM)
                o_new, l_new = jnp.concatenate(outs[hh], axis=1), jnp.concatenate(lses[hh], axis=1)
                if g > 0:
                    o_old, l_old = o_scr[c, tok, :], l_scr[c, tok, :]
                    m = jnp.maximum(l_old, l_new)
                    e_old, e_new = jnp.exp(l_old - m), jnp.exp(l_new - m)
                    o_new = (e_old * o_old + e_new * o_new) / (e_old + e_new)
                    l_new = m + jnp.log(e_old + e_new)
                o_scr[c, tok, :] = o_new
                l_scr[c, tok, :] = l_new
    for c in range(MIX_WIDTH // LANES):
        o_ref[:, c * LANES:(c + 1) * LANES] = o_scr[c]


def _attention(at, batch, seq):
    T = batch * seq
    nblk = seq // ATT_TOKENS
    args, specs = [], []
    for a, (_, d) in zip(at, DIL_PAIRS):
        rows = ATT_TOKENS // d
        n_prev = rows // ATT_BLOCK
        a3 = a.reshape(batch, seq // d, d * QKV_WIDTH)
        args += [a3, a3]
        specs += [pl.BlockSpec((1, rows, d * QKV_WIDTH), lambda b, i: (b, i, 0)),
                  pl.BlockSpec((1, ATT_BLOCK, d * QKV_WIDTH), lambda b, i, n=n_prev: (b, jnp.maximum(i * n - 1, 0), 0))]
    return pl.pallas_call(
        _attn_kernel,
        grid=(batch, nblk),
        in_specs=specs,
        out_specs=pl.BlockSpec((ATT_TOKENS, MIX_WIDTH), lambda b, i: (b * nblk + i, 0)),
        out_shape=jax.ShapeDtypeStruct((T, MIX_WIDTH), F32),
        scratch_shapes=[pltpu.VMEM((MIX_WIDTH // LANES, ATT_TOKENS, LANES), F32)] * 2,
        compiler_params=_params(2),
        name="dilated_attn",
    )(*args)


def _merge_kernel(x_ref, nw_ref, wg_ref, oa_ref, ob_ref, oc_ref, od_ref, wb_ref, wo_ref, out_ref):
    x = x_ref[...]
    hb = _rms(x, nw_ref[...]).astype(BF16)
    acc = jnp.zeros(x.shape, F32)
    for kbr, o in enumerate((oa_ref[...], ob_ref[...], oc_ref[...], od_ref[...])):
        gate = _sigmoid(jnp.dot(hb, wg_ref[:, kbr * D_MODEL:(kbr + 1) * D_MODEL], preferred_element_type=F32))
        acc = acc + gate * _mm(o, wb_ref[kbr])
    out_ref[...] = x + _mm(acc, wo_ref[...])


def _merge(x2, nw, w_gate, o_a, o_b, o_c, o_d, w_branch, w_out):
    T, D = x2.shape
    tm = ROW_TILE
    row = lambda w: pl.BlockSpec((tm, w), lambda i: (i, 0))
    return pl.pallas_call(
        _merge_kernel,
        grid=(T // tm,),
        in_specs=[row(D), _const_spec((1, D)), _const_spec(w_gate.shape)] + [row(MIX_WIDTH)] * N_BRANCH
                 + [_const_spec(w_branch.shape), _const_spec(w_out.shape)],
        out_specs=row(D),
        out_shape=jax.ShapeDtypeStruct((T, D), F32),
        compiler_params=_params(1),
        name="merge",
    )(x2, nw, w_gate, o_a, o_b, o_c, o_d, w_branch, w_out)


def _ffn_kernel(x_ref, nw_ref, wup_ref, cw_ref, cb_ref, wdn_ref, fw_ref, out_ref, tail_ref, *, n_s, final_norm):
    tm = x_ref.shape[0]

    @pl.when(pl.program_id(0) % n_s == 0)
    def _():
        tail_ref[...] = jnp.zeros_like(tail_ref)

    x = x_ref[...]
    hb = _rms(x, nw_ref[...]).astype(BF16)
    up = jnp.dot(hb, wup_ref[:, :D_FF], preferred_element_type=F32)
    gate = jnp.dot(hb, wup_ref[:, D_FF:], preferred_element_type=F32)
    row = lax.broadcasted_iota(jnp.int32, up.shape, 0)
    tail = tail_ref[...]
    up1 = jnp.where(row == 0, tail[1:2, :], pltpu.roll(up, 1, 0))
    up2 = jnp.where(row == 0, tail[0:1, :], jnp.where(row == 1, tail[1:2, :], pltpu.roll(up, 2, 0)))
    tail_ref[...] = up[tm - 2:tm, :]
    cw = cw_ref[...]
    conv = cb_ref[...] + cw[0:1, :] * up + cw[1:2, :] * up1 + cw[2:3, :] * up2
    act = conv * _sigmoid(conv) * gate
    y = x + _mm(act, wdn_ref[...])
    out_ref[...] = _rms(y, fw_ref[...]) if final_norm else y


def _ffn(x2, nw, w_up, conv_w, conv_b, w_down, final_w, seq, final_norm):
    T, D = x2.shape
    tm = ROW_TILE
    row = pl.BlockSpec((tm, D), lambda i: (i, 0))
    return pl.pallas_call(
        functools.partial(_ffn_kernel, n_s=seq // tm, final_norm=final_norm),
        grid=(T // tm,),
        in_specs=[row, _const_spec((1, D)), _const_spec(w_up.shape), _const_spec(conv_w.shape),
                  _const_spec((1, D_FF)), _const_spec(w_down.shape), _const_spec((1, D))],
        out_specs=row,
        out_shape=jax.ShapeDtypeStruct((T, D), F32),
        scratch_shapes=[pltpu.VMEM((CONV_W - 1, D_FF), F32)],
        compiler_params=_params(1),
        name="convglu",
    )(x2, nw, w_up, conv_w, conv_b, w_down, final_w)


def kernel(x, norm_mix_w, norm_ffn_w, norm_final_w, w_in, hg_lb_table, hg_norm_w, gla_a_w2, gla_a_b, gla_norm_w, rw_mu, rw_w0, rw_w2, rw_a0, rw_a2, rw_g2, rw_k_k, rw_k_a, rw_r_k, rw_ln_w, rw_ln_b, rw_v0, rw_v1, rw_v2, w_branch, w_out, ffn_w_up, ffn_conv_w, ffn_conv_b, ffn_w_down):
    B, S, D = x.shape
    T = B * S
    x2 = x.reshape(T, D)
    cos_t, sin_t = _rope_tables(S)
    row = lambda t: t.reshape(1, -1)
    W = MIX_WIDTH
    src = np.concatenate([np.arange(0, W), np.arange(W + RW_DECAY_LORA, 3 * W + RW_DECAY_LORA),
                          np.arange(W, W + RW_DECAY_LORA), np.arange(3 * W + RW_DECAY_LORA, RW_WIDTH)])
    at_src = np.concatenate([np.arange(j * AT_WIDTH + g * W, j * AT_WIDTH + (g + 1) * W)
                             for g in range(N_DIL) for j in range(3)])
    o_hg, o_gla, o_rw, o_at = 0, HG_WIDTH, HG_WIDTH + GLA_WIDTH, HG_WIDTH + GLA_WIDTH + RW_WIDTH
    o_gate = o_at + 3 * AT_WIDTH
    v_first = None
    for layer in range(DEPTH):
        w = w_in[layer]
        w_hg = w[:, o_hg:o_gla].astype(BF16)
        w_gla = w[:, o_gla:o_rw].astype(BF16)
        w_rw = w[:, o_rw:o_at][:, src].astype(BF16)
        w_at = w[:, o_at:o_gate][:, at_src].astype(BF16)
        w_gate = w[:, o_gate:].astype(BF16)
        z_hg, z_gla, z_rw, *at = _inproj(x2, row(norm_mix_w[layer]), cos_t, sin_t, w_hg, w_gla, w_rw, w_at, S)

        o_a = _hgrn2(z_hg, hg_lb_table, row(hg_norm_w[layer]), layer, S)
        o_b = _gla(z_gla, gla_a_w2[layer].astype(BF16), row(gla_a_b[layer]), row(gla_norm_w[layer]), S)
        p = {"mu": row(rw_mu[layer][src]), "w0": row(rw_w0[layer]), "w2": rw_w2[layer].astype(BF16),
             "a0": row(rw_a0[layer]), "a2": rw_a2[layer].astype(BF16), "g2": rw_g2[layer].astype(BF16),
             "k_k": row(rw_k_k[layer]), "k_a": row(rw_k_a[layer]), "r_k": row(rw_r_k[layer]),
             "ln_w": row(rw_ln_w[layer]), "ln_b": row(rw_ln_b[layer])}
        if layer > 0:
            p.update(v0=row(rw_v0[layer - 1]), v1=rw_v1[layer - 1].astype(BF16), v2=rw_v2[layer - 1].astype(BF16))
        o_c, v_new = _rwkv(z_rw, v_first, p, S)
        if layer == 0:
            v_first = v_new
        o_d = _attention(at, B, S)

        x2 = _merge(x2, row(norm_mix_w[layer]), w_gate, o_a, o_b, o_c, o_d,
                    w_branch[layer].astype(BF16), w_out[layer].astype(BF16))
        x2 = _ffn(x2, row(norm_ffn_w[layer]), ffn_w_up[layer].astype(BF16), ffn_conv_w[layer],
                  row(ffn_conv_b[layer]), ffn_w_down[layer].astype(BF16), row(norm_final_w), S,
                  final_norm=(layer == DEPTH - 1))
    return x2.reshape(B, S, D)
```

```python
import functools

import numpy as np
import jax
import jax.numpy as jnp
from jax import lax
from jax.experimental import pallas as pl
from jax.experimental.pallas import tpu as pltpu

F32 = jnp.float32
BF16 = jnp.bfloat16

D_MODEL = 1024
DEPTH = 2
HEAD_DIM = 64
MIX_WIDTH = D_MODEL // 4
N_BRANCH = 4
NORM_EPS = 1e-6
MASK_VALUE = -1e9
EXP_CLIP = 60.0
N_HEADS = MIX_WIDTH // HEAD_DIM
GLA_DK = 32
GLA_RANK = 16
GLA_TAU = 16.0
RW_DECAY_LORA = 32
RW_AAA_LORA = 32
RW_MV_LORA = 32
RW_GATE_LORA = 64
RW_GN_EPS = 64e-5
DIL_PAIRS = ((128, 1), (512, 4), (2048, 16))
N_DIL = 3
AT_WIDTH = N_DIL * N_HEADS * HEAD_DIM
ROPE_THETA = 10000.0
D_FF = 11 * D_MODEL // 4
CONV_W = 3

HG_WIDTH = 4 * MIX_WIDTH
GLA_QK = N_HEADS * GLA_DK
GLA_WIDTH = 2 * GLA_QK + 2 * MIX_WIDTH + GLA_RANK
RW_WIDTH = 3 * MIX_WIDTH + RW_DECAY_LORA + RW_AAA_LORA + RW_GATE_LORA
GATE_WIDTH = N_BRANCH * D_MODEL

LANES = 128
VMEM_LIMIT_BYTES = 56 * 1024 * 1024
ROW_TILE = 256
TIME_BLOCK = 512
GLA_CHUNK = 128
RW_CHUNK = 64
ATT_BLOCK = 128


def _const_spec(shape):
    nd = len(shape)
    return pl.BlockSpec(shape, lambda *_: (0,) * nd, pipeline_mode=pl.Buffered(1))


def _params(n_grid):
    return pltpu.CompilerParams(dimension_semantics=("arbitrary",) * n_grid, vmem_limit_bytes=VMEM_LIMIT_BYTES)


def _mm(a, b):
    return jnp.dot(a.astype(BF16), b.astype(BF16), preferred_element_type=F32)


def _mm_nt(a, b):
    return lax.dot_general(a.astype(BF16), b.astype(BF16), (((1,), (1,)), ((), ())), preferred_element_type=F32)


def _mm_tn(a, b):
    return lax.dot_general(a.astype(BF16), b.astype(BF16), (((0,), (0,)), ((), ())), preferred_element_type=F32)


def _split2(a):
    hi = a.astype(BF16)
    lo = (a - hi.astype(F32)).astype(BF16)
    return hi, lo


def _mm_sel_lhs(sel, a):
    hi, lo = _split2(a)
    return jnp.dot(sel, hi, preferred_element_type=F32) + jnp.dot(sel, lo, preferred_element_type=F32)


def _mm_sel_rhs(a, sel):
    hi, lo = _split2(a)
    return jnp.dot(hi, sel, preferred_element_type=F32) + jnp.dot(lo, sel, preferred_element_type=F32)


def _rms(x, w):
    return x * lax.rsqrt(jnp.mean(x * x, axis=-1, keepdims=True) + NORM_EPS) * w


def _sigmoid(x):
    return jax.nn.sigmoid(x)


def _log_sigmoid(x):
    return jnp.minimum(x, 0.0) - jnp.log1p(jnp.exp(-jnp.abs(x)))


def _softplus(x):
    return jnp.maximum(x, 0.0) + jnp.log1p(jnp.exp(-jnp.abs(x)))


def _n_levels(chunk):
    return int(np.log2(chunk))


def _prefix_matrix(chunk):
    C, L = chunk, _n_levels(chunk)
    M = np.zeros(((L + 1) * C, C), np.float32)
    M[:C] = np.tril(np.ones((C, C), np.float32))
    for l in range(L):
        m = C >> (l + 1)
        for t in range(C):
            mid = (t // (2 * m)) * 2 * m + m - 1
            if t % (2 * m) >= m:
                M[(l + 1) * C + t, mid + 1:t + 1] = 1.0
            else:
                M[(l + 1) * C + t, t + 1:mid + 1] = 1.0
    return jnp.asarray(M, BF16)


def _block_ones(n_groups, w_in, w_out):
    return jnp.asarray(np.kron(np.eye(n_groups, dtype=np.float32), np.ones((w_in, w_out), np.float32)), BF16)


def _rope_tables(seq):
    half = HEAD_DIM // 2
    inv = ROPE_THETA ** (-np.arange(half, dtype=np.float64) / half)
    ang = np.arange(seq, dtype=np.float64)[:, None] * inv[None, :]
    cos, sin = np.cos(ang), np.sin(ang)
    cos_h = np.concatenate([cos, cos], axis=1)
    sin_h = np.concatenate([-sin, sin], axis=1)
    return (jnp.asarray(np.tile(cos_h, (1, N_HEADS)), F32), jnp.asarray(np.tile(sin_h, (1, N_HEADS)), F32))


QKV_WIDTH = 3 * MIX_WIDTH


def _inproj_kernel(x_ref, nw_ref, cos_ref, sin_ref, whg_ref, wgla_ref, wrw_ref, wat_ref,
                   zhg_ref, zgla_ref, zrw_ref, at1_ref, at4_ref, at16_ref, at_scr):
    tm = x_ref.shape[0]
    hb = _rms(x_ref[...], nw_ref[...]).astype(BF16)
    zhg_ref[...] = jnp.dot(hb, whg_ref[...], preferred_element_type=F32)
    zgla_ref[...] = jnp.dot(hb, wgla_ref[...], preferred_element_type=F32)
    zrw_ref[...] = jnp.dot(hb, wrw_ref[...], preferred_element_type=F32)
    zat = jnp.dot(hb, wat_ref[...], preferred_element_type=F32)
    cos, sin = cos_ref[...], sin_ref[...]
    lane = lax.broadcasted_iota(jnp.int32, cos.shape, 1)
    first_half = (lane % HEAD_DIM) < (HEAD_DIM // 2)

    def rope(p):
        partner = jnp.where(first_half, pltpu.roll(p, MIX_WIDTH - HEAD_DIM // 2, 1), pltpu.roll(p, HEAD_DIM // 2, 1))
        return p * cos + partner * sin

    n_lane_blk = QKV_WIDTH // LANES
    for g, (out_ref, (_, dil)) in enumerate(zip((at1_ref, at4_ref, at16_ref), DIL_PAIRS)):
        c0 = g * QKV_WIDTH
        q = rope(zat[:, c0:c0 + MIX_WIDTH]) * (HEAD_DIM ** -0.5)
        k = rope(zat[:, c0 + MIX_WIDTH:c0 + 2 * MIX_WIDTH])
        qkv = jnp.concatenate([q, k, zat[:, c0 + 2 * MIX_WIDTH:c0 + QKV_WIDTH]], axis=1)
        if dil == 1:
            out_ref[...] = qkv.astype(BF16)
            continue
        for c in range(n_lane_blk):
            at_scr[c] = qkv[:, c * LANES:(c + 1) * LANES]
        for r in range(dil):
            rows = jnp.concatenate([at_scr[c, pl.ds(r, tm // dil, stride=dil), :] for c in range(n_lane_blk)], axis=1)
            out_ref[:, r * QKV_WIDTH:(r + 1) * QKV_WIDTH] = rows.astype(BF16)


def _inproj(x2, nw, cos_t, sin_t, w_hg, w_gla, w_rw, w_at, seq):
    T, D = x2.shape
    tm = ROW_TILE
    n_s = seq // tm
    row = lambda w: pl.BlockSpec((tm, w), lambda i: (i, 0))
    pos = pl.BlockSpec((tm, MIX_WIDTH), lambda i: (i % n_s, 0))
    widths = (HG_WIDTH, GLA_WIDTH, RW_WIDTH)
    dils = [d for _, d in DIL_PAIRS]
    return pl.pallas_call(
        _inproj_kernel,
        grid=(T // tm,),
        in_specs=[row(D), _const_spec((1, D)), pos, pos,
                  _const_spec(w_hg.shape), _const_spec(w_gla.shape), _const_spec(w_rw.shape), _const_spec(w_at.shape)],
        out_specs=[row(w) for w in widths]
                  + [pl.BlockSpec((tm // d, d * QKV_WIDTH), lambda i: (i, 0)) for d in dils],
        out_shape=[jax.ShapeDtypeStruct((T, w), F32) for w in widths]
                  + [jax.ShapeDtypeStruct((T // d, d * QKV_WIDTH), BF16) for d in dils],
        scratch_shapes=[pltpu.VMEM((QKV_WIDTH // LANES, tm, LANES), F32)],
        compiler_params=_params(1),
        name="inproj",
    )(x2, nw, cos_t, sin_t, w_hg, w_gla, w_rw, w_at)


def _level_masks(chunk):
    t = lax.broadcasted_iota(jnp.int32, (chunk, chunk), 0)
    s = lax.broadcasted_iota(jnp.int32, (chunk, chunk), 1)
    masks = []
    for l in range(_n_levels(chunk)):
        m = chunk >> (l + 1)
        sh = int(np.log2(2 * m))
        masks.append(((t >> sh) == (s >> sh)) & ((t & m) != 0) & ((s & m) == 0))
    return masks


def _gla_block(q, k, v, g, st_ref, pm_ref, qk_ones, *, chunk, dk, dv):
    C = chunk
    nc = q.shape[0] // C
    n_lvl = _n_levels(C)
    masks = _level_masks(C)
    pm = pm_ref[...]
    rows = lambda t, c: t[c * C:(c + 1) * C]
    stack = lambda f: jnp.concatenate([f(c) for c in range(nc)], axis=0)
    D = [_mm_sel_lhs(pm, rows(g, c)) for c in range(nc)]
    b = stack(lambda c: D[c][:C])
    b_last = stack(lambda c: jnp.broadcast_to(D[c][C - 1:C, :], (C, b.shape[1])))
    dec_last = [jnp.exp(D[c][C - 1:C, :]) for c in range(nc)]
    q_in = (q * jnp.exp(b)).astype(BF16)
    k_dec = (k * jnp.exp(b_last - b)).astype(BF16)
    vb = v.astype(BF16)
    qf, kf = [], []
    for l in range(n_lvl):
        e = jnp.exp(stack(lambda c: D[c][(l + 1) * C:(l + 2) * C]))
        qf.append((q * e).astype(BF16))
        kf.append((k * e).astype(BF16))
    diag = _mm(q * k, qk_ones)
    pairs = [(c, h) for c in range(nc) for h in range(N_HEADS)]
    bk = lambda t, c, h: t[c * C:(c + 1) * C, h * dk:(h + 1) * dk]
    bv = lambda t, c, h: t[c * C:(c + 1) * C, h * dv:(h + 1) * dv]
    nt = (((1,), (1,)), ((), ()))
    tn = (((0,), (0,)), ((), ()))
    prod = {(p, l): lax.dot_general(bk(qf[l], *p), bk(kf[l], *p), nt, preferred_element_type=F32)
            for p in pairs for l in range(n_lvl)}
    o_intra, kv = {}, {}
    for p in pairs:
        sc = jnp.zeros((C, C), F32)
        for l in range(n_lvl):
            sc = jnp.where(masks[l], prod[p, l], sc)
        o_intra[p] = jnp.dot(sc.astype(BF16), bv(vb, *p), preferred_element_type=F32)
        kv[p] = lax.dot_general(bv(vb, *p), bk(k_dec, *p), tn, preferred_element_type=F32)
    s = [st_ref[h] for h in range(N_HEADS)]
    out_rows = []
    for c in range(nc):
        o_inter = [lax.dot_general(bk(q_in, c, h), s[h].astype(BF16), nt, preferred_element_type=F32)
                   for h in range(N_HEADS)]
        s = [s[h] * dec_last[c][:, h * dk:(h + 1) * dk] + kv[c, h] for h in range(N_HEADS)]
        out_rows.append(jnp.concatenate([o_intra[c, h] + o_inter[h] for h in range(N_HEADS)], axis=1))
    for h in range(N_HEADS):
        st_ref[h] = s[h]
    return jnp.concatenate(out_rows, axis=0) + diag * v


def _head_rmsnorm(o, head_ones, w):
    ms = _mm_sel_rhs(o * o, head_ones) * (1.0 / HEAD_DIM)
    return o * lax.rsqrt(ms + NORM_EPS) * w


def _hgrn2_kernel(z_ref, lbt_ref, nw_ref, pm_ref, ones_ref, o_ref, st_ref, *, layer):
    @pl.when(pl.program_id(1) == 0)
    def _():
        st_ref[...] = jnp.zeros_like(st_ref)

    tab = lbt_ref[...]
    e = jnp.exp(tab - jnp.max(tab, axis=0, keepdims=True))
    p = e / jnp.sum(e, axis=0, keepdims=True)
    lb = jnp.zeros((1, MIX_WIDTH), F32)
    for i in range(1, layer + 1):
        lb = lb + p[i:i + 1, :]
    ones = ones_ref[...]
    z = z_ref[...]
    q_raw, fz = z[:, :MIX_WIDTH], z[:, MIX_WIDTH:2 * MIX_WIDTH]
    i_raw, g_raw = z[:, 2 * MIX_WIDTH:3 * MIX_WIDTH], z[:, 3 * MIX_WIDTH:]
    log_f = _log_sigmoid(fz) + jnp.log1p(lb * jnp.exp(jnp.minimum(-fz, EXP_CLIP)))
    k = (1.0 - lb) * _sigmoid(-fz)
    q = q_raw * _sigmoid(q_raw)
    o = _gla_block(q, k, i_raw, log_f, st_ref, pm_ref, ones, chunk=GLA_CHUNK, dk=HEAD_DIM, dv=HEAD_DIM)
    o_ref[...] = _head_rmsnorm(o, ones, nw_ref[...]) * _sigmoid(g_raw)


def _gla_kernel(z_ref, aw2_ref, ab_ref, nw_ref, pm_ref, qk_ones_ref, ones_ref, o_ref, st_ref):
    @pl.when(pl.program_id(1) == 0)
    def _():
        st_ref[...] = jnp.zeros_like(st_ref)

    z = z_ref[...]
    q = z[:, :GLA_QK] * (GLA_DK ** -0.5)
    k = z[:, GLA_QK:2 * GLA_QK]
    v = z[:, 2 * GLA_QK:2 * GLA_QK + MIX_WIDTH]
    gate = z[:, 2 * GLA_QK + MIX_WIDTH:2 * GLA_QK + 2 * MIX_WIDTH]
    a_low = z[:, 2 * GLA_QK + 2 * MIX_WIDTH:]
    log_alpha = _log_sigmoid(_mm(a_low, aw2_ref[...]) + ab_ref[...]) * (1.0 / GLA_TAU)
    o = _gla_block(q, k, v, log_alpha, st_ref, pm_ref, qk_ones_ref[...], chunk=GLA_CHUNK, dk=GLA_DK, dv=HEAD_DIM)
    o_ref[...] = _head_rmsnorm(o, ones_ref[...], nw_ref[...]) * (gate * _sigmoid(gate))


def _time_grid(T, seq):
    ts = TIME_BLOCK
    n_t = seq // ts
    return ts, n_t, (T // seq, n_t)


def _hgrn2(z_hg, lb_table, norm_w, layer, seq):
    T = z_hg.shape[0]
    ts, n_t, grid = _time_grid(T, seq)
    blk = lambda w: pl.BlockSpec((ts, w), lambda b, i: (b * n_t + i, 0))
    pm = _prefix_matrix(GLA_CHUNK)
    ones = _block_ones(N_HEADS, HEAD_DIM, HEAD_DIM)
    return pl.pallas_call(
        functools.partial(_hgrn2_kernel, layer=layer),
        grid=grid,
        in_specs=[blk(HG_WIDTH), _const_spec(lb_table.shape), _const_spec((1, MIX_WIDTH)),
                  _const_spec(pm.shape), _const_spec(ones.shape)],
        out_specs=blk(MIX_WIDTH),
        out_shape=jax.ShapeDtypeStruct((T, MIX_WIDTH), F32),
        scratch_shapes=[pltpu.VMEM((N_HEADS, HEAD_DIM, HEAD_DIM), F32)],
        compiler_params=_params(2),
        name="hgrn2",
    )(z_hg, lb_table, norm_w, pm, ones)


def _gla(z_gla, a_w2, a_b, norm_w, seq):
    T = z_gla.shape[0]
    ts, n_t, grid = _time_grid(T, seq)
    blk = lambda w: pl.BlockSpec((ts, w), lambda b, i: (b * n_t + i, 0))
    pm = _prefix_matrix(GLA_CHUNK)
    qk_ones = _block_ones(N_HEADS, GLA_DK, HEAD_DIM)
    ones = _block_ones(N_HEADS, HEAD_DIM, HEAD_DIM)
    return pl.pallas_call(
        _gla_kernel,
        grid=grid,
        in_specs=[blk(GLA_WIDTH), _const_spec(a_w2.shape), _const_spec((1, GLA_QK)), _const_spec((1, MIX_WIDTH)),
                  _const_spec(pm.shape), _const_spec(qk_ones.shape), _const_spec(ones.shape)],
        out_specs=blk(MIX_WIDTH),
        out_shape=jax.ShapeDtypeStruct((T, MIX_WIDTH), F32),
        scratch_shapes=[pltpu.VMEM((N_HEADS, HEAD_DIM, GLA_DK), F32)],
        compiler_params=_params(2),
        name="gla",
    )(z_gla, a_w2, a_b, norm_w, pm, qk_ones, ones)


_RW_R, _RW_K, _RW_V = 0, MIX_WIDTH, 2 * MIX_WIDTH
_RW_WL = 3 * MIX_WIDTH
_RW_AL = _RW_WL + RW_DECAY_LORA
_RW_GL = _RW_AL + RW_AAA_LORA


def _rwkv_kernel(*refs, has_vfirst):
    if has_vfirst:
        (z_ref, vf_ref, mu_ref, w0_ref, w2_ref, a0_ref, a2_ref, g2_ref, kk_ref, ka_ref, rk_ref, lnw_ref, lnb_ref,
         v0_ref, v1_ref, v2_ref, tri_ref, ones_ref, o_ref, s_ref, carry_ref) = refs
    else:
        (z_ref, mu_ref, w0_ref, w2_ref, a0_ref, a2_ref, g2_ref, kk_ref, ka_ref, rk_ref, lnw_ref, lnb_ref,
         tri_ref, ones_ref, o_ref, vout_ref, s_ref, carry_ref) = refs
    C, HD = RW_CHUNK, HEAD_DIM
    ts = z_ref.shape[0]
    nc = ts // C
    pairs = [(c, h) for c in range(nc) for h in range(N_HEADS)]

    @pl.when(pl.program_id(1) == 0)
    def _():
        s_ref[...] = jnp.zeros_like(s_ref)
        carry_ref[...] = jnp.zeros_like(carry_ref)

    z = z_ref[...]
    zp = pltpu.roll(z, 1, 0)
    zp = jnp.where(lax.broadcasted_iota(jnp.int32, z.shape, 0) == 0, carry_ref[...], zp)
    carry_ref[...] = z[ts - 1:ts, :]
    zs = z + mu_ref[...] * (zp - z)
    r, k, v = zs[:, _RW_R:_RW_K], zs[:, _RW_K:_RW_V], zs[:, _RW_V:_RW_WL]
    wl, al, gl = zs[:, _RW_WL:_RW_AL], zs[:, _RW_AL:_RW_GL], zs[:, _RW_GL:]
    ones = ones_ref[...]
    gsum = lambda t: _mm_sel_rhs(t, ones)
    w = -_softplus(-(w0_ref[...] + _mm(jnp.tanh(wl), w2_ref[...]))) - 0.5
    lw = -jnp.exp(w)
    a = _sigmoid(a0_ref[...] + _mm(al, a2_ref[...]))
    g = _mm(_sigmoid(gl), g2_ref[...])
    if has_vfirst:
        v = v + (vf_ref[...] - v) * _sigmoid(v0_ref[...] + _mm(_mm(v, v1_ref[...]), v2_ref[...]))
    else:
        vout_ref[...] = v
    kk = k * kk_ref[...]
    kk = kk / jnp.maximum(jnp.sqrt(gsum(kk * kk)), 1e-12)
    k_mod = k * (1.0 + (a - 1.0) * ka_ref[...])
    kka = kk * a

    tri = tri_ref[...]
    cums = [_mm_sel_lhs(tri, lw[c * C:(c + 1) * C]) for c in range(nc)]
    cum = jnp.concatenate(cums, axis=0)
    cum_last = jnp.concatenate([jnp.broadcast_to(cc[C - 1:C, :], (C, MIX_WIDTH)) for cc in cums], axis=0)
    w_end = [jnp.exp(cc[C - 1:C, :]) for cc in cums]
    r_t = r * jnp.exp(cum)
    a_t = -kk * jnp.exp(cum - lw)
    inv = jnp.exp(-cum)
    end = jnp.exp(cum_last - cum)
    r_tb, a_tb = r_t.astype(BF16), a_t.astype(BF16)
    k_hb, b_hb = (k_mod * inv).astype(BF16), (kka * inv).astype(BF16)
    k_eb, b_eb = (k_mod * end).astype(BF16), (kka * end).astype(BF16)
    vb = v.astype(BF16)
    blk = lambda t, c, h: t[c * C:(c + 1) * C, h * HD:(h + 1) * HD]

    ti = lax.broadcasted_iota(jnp.int32, (C, C), 0)
    si = lax.broadcasted_iota(jnp.int32, (C, C), 1)
    incl, strict = si <= ti, si < ti
    dot = lambda x, y: jnp.dot(x, y, preferred_element_type=F32)
    aa = {p: _mm_nt(jnp.concatenate([blk(r_tb, *p), blk(a_tb, *p)], axis=0),
                    jnp.concatenate([blk(k_hb, *p), blk(b_hb, *p)], axis=0)) for p in pairs}
    a_r = {p: jnp.concatenate([jnp.where(incl, aa[p][:C, :C], 0.0), jnp.where(incl, aa[p][:C, C:], 0.0)],
                              axis=1).astype(BF16) for p in pairs}
    a_ak = {p: jnp.where(strict, aa[p][C:, :C], 0.0).astype(BF16) for p in pairs}
    npow = {p: jnp.where(strict, aa[p][C:, C:], 0.0) for p in pairs}
    pv = {p: dot(a_ak[p], blk(vb, *p)) for p in pairs}
    x = {p: jnp.concatenate([blk(a_t, *p), pv[p]], axis=1) for p in pairs}
    n_stage = _n_levels(C)
    for j in range(n_stage):
        last = j == n_stage - 1
        prod = {}
        for p in pairs:
            nb, xb = npow[p].astype(BF16), x[p].astype(BF16)
            prod[p] = dot(nb, xb if last else jnp.concatenate([nb, xb], axis=1))
        for p in pairs:
            if last:
                x[p] = x[p] + prod[p]
            else:
                npow[p] = prod[p][:, :C]
                x[p] = x[p] + prod[p][:, C:]
    rmat = {p: jnp.concatenate([jnp.concatenate([jnp.zeros((C, HD), BF16), blk(vb, *p)], axis=1), x[p].astype(BF16)],
                               axis=0) for p in pairs}
    qz = {p: dot(a_r[p], rmat[p]) for p in pairs}
    gh = {p: lax.dot_general(rmat[p], jnp.concatenate([blk(k_eb, *p), blk(b_eb, *p)], axis=0),
                             (((0,), (0,)), ((), ())), preferred_element_type=F32) for p in pairs}
    q_f = {p: (blk(r_t, *p) + qz[p][:, :HD]).astype(BF16) for p in pairs}

    s = [s_ref[h] for h in range(N_HEADS)]
    y_rows = []
    for c in range(nc):
        sb = [s[h].astype(BF16) for h in range(N_HEADS)]
        ys = [lax.dot_general(q_f[c, h], sb[h], (((1,), (1,)), ((), ())), preferred_element_type=F32)
              + qz[c, h][:, HD:] for h in range(N_HEADS)]
        sg = [dot(sb[h], gh[c, h][:HD].astype(BF16)) for h in range(N_HEADS)]
        s = [s[h] * w_end[c][:, h * HD:(h + 1) * HD] + sg[h] + gh[c, h][HD:] for h in range(N_HEADS)]
        y_rows.append(jnp.concatenate(ys, axis=1))
    for h in range(N_HEADS):
        s_ref[h] = s[h]
    y = jnp.concatenate(y_rows, axis=0)

    mean = gsum(y) * (1.0 / HD)
    yc = y - mean
    var = gsum(yc * yc) * (1.0 / HD)
    yn = yc * lax.rsqrt(var + RW_GN_EPS) * lnw_ref[...] + lnb_ref[...]
    bonus = gsum(r * k_mod * rk_ref[...]) * v
    o_ref[...] = (yn + bonus) * g


def _rwkv(z_rw, v_first, p, seq):
    T = z_rw.shape[0]
    ts, n_t, grid = _time_grid(T, seq)
    blk = lambda w: pl.BlockSpec((ts, w), lambda b, i: (b * n_t + i, 0))
    tri = jnp.asarray(np.tril(np.ones((RW_CHUNK, RW_CHUNK), np.float32)), BF16)
    ones = _block_ones(N_HEADS, HEAD_DIM, HEAD_DIM)
    has_vfirst = v_first is not None
    names = ["mu", "w0", "w2", "a0", "a2", "g2", "k_k", "k_a", "r_k", "ln_w", "ln_b"]
    args = [z_rw] + ([v_first] if has_vfirst else []) + [p[n] for n in names]
    specs = [blk(RW_WIDTH)] + ([blk(MIX_WIDTH)] if has_vfirst else []) + [_const_spec(p[n].shape) for n in names]
    if has_vfirst:
        args += [p["v0"], p["v1"], p["v2"]]
        specs += [_const_spec(p[n].shape) for n in ("v0", "v1", "v2")]
    args += [tri, ones]
    specs += [_const_spec(tri.shape), _const_spec(ones.shape)]
    out_sds = jax.ShapeDtypeStruct((T, MIX_WIDTH), F32)
    res = pl.pallas_call(
        functools.partial(_rwkv_kernel, has_vfirst=has_vfirst),
        grid=grid,
        in_specs=specs,
        out_specs=blk(MIX_WIDTH) if has_vfirst else [blk(MIX_WIDTH), blk(MIX_WIDTH)],
        out_shape=out_sds if has_vfirst else [out_sds, out_sds],
        scratch_shapes=[pltpu.VMEM((N_HEADS, HEAD_DIM, HEAD_DIM), F32), pltpu.VMEM((1, RW_WIDTH), F32)],
        compiler_params=_params(2),
        name="rwkv7",
    )(*args)
    return (res, None) if has_vfirst else (res[0], res[1])


ATT_TOKENS = ATT_BLOCK * max(d for _, d in DIL_PAIRS)
ATT_UNITS_PER_BATCH = 8


def _attn_kernel(c1_ref, p1_ref, c4_ref, p4_ref, c16_ref, p16_ref, o_ref, acc_scr, m_scr, l_scr):
    qb = ATT_BLOCK
    n_tile = MIX_WIDTH // LANES
    row = lax.broadcasted_iota(jnp.int32, (qb, 2 * qb), 0)
    col = lax.broadcasted_iota(jnp.int32, (qb, 2 * qb), 1)
    band = (col >= row) & (col <= row + qb)
    band_first = band & (col >= jnp.where(pl.program_id(1) > 0, 0, qb))
    low = lax.broadcasted_iota(jnp.int32, (qb, LANES), 1) < HEAD_DIM
    nt = (((1,), (1,)), ((), ()))
    curs, prevs = (c1_ref, c4_ref, c16_ref), (p1_ref, p4_ref, p16_ref)
    units = [(g, dil, r, qi) for g, (_, dil) in enumerate(DIL_PAIRS) for r in range(dil)
             for qi in range(ATT_TOKENS // dil // qb)]
    pairs = [(t, e) for t in range(n_tile) for e in range(LANES // HEAD_DIM)]
    tile = lambda x, t: x[:, t * LANES:(t + 1) * LANES]
    for b0 in range(0, len(units), ATT_UNITS_PER_BATCH):
        batch = units[b0:b0 + ATT_UNITS_PER_BATCH]
        vs, masks, scores = [], [], []
        for g, dil, r, qi in batch:
            cur, prev = curs[g], prevs[g]
            qc, kc, vc = (r * QKV_WIDTH + j * MIX_WIDTH for j in range(3))
            q = cur[0, qi * qb:(qi + 1) * qb, qc:qc + MIX_WIDTH]
            if qi == 0:
                k = jnp.concatenate([prev[0, :, kc:kc + MIX_WIDTH], cur[0, :qb, kc:kc + MIX_WIDTH]], axis=0)
                v = jnp.concatenate([prev[0, :, vc:vc + MIX_WIDTH], cur[0, :qb, vc:vc + MIX_WIDTH]], axis=0)
            else:
                k = cur[0, (qi - 1) * qb:(qi + 1) * qb, kc:kc + MIX_WIDTH]
                v = cur[0, (qi - 1) * qb:(qi + 1) * qb, vc:vc + MIX_WIDTH]
            vs.append(v)
            masks.append(band_first if qi == 0 else band)
            zero = jnp.zeros((qb, LANES), q.dtype)
            scores.append({(t, e): lax.dot_general(jnp.where(low if e == 0 else ~low, tile(q, t), zero), tile(k, t), nt,
                                                   preferred_element_type=F32) for t, e in pairs})
        probs, stats = [], []
        for u in range(len(batch)):
            pu, su = {}, {}
            for p in pairs:
                s = jnp.where(masks[u], scores[u][p], MASK_VALUE)
                m = jnp.max(s, axis=-1, keepdims=True)
                ex = jnp.exp(s - m)
                su[p] = (m, jnp.sum(ex, axis=-1, keepdims=True))
                pu[p] = ex.astype(BF16)
            probs.append(pu)
            stats.append(su)
        pvs = [{(t, e): jnp.dot(probs[u][t, e], tile(vs[u], t), preferred_element_type=F32) for t, e in pairs}
               for u in range(len(batch))]
        for u, (g, dil, r, qi) in enumerate(batch):
            tok = pl.ds(qi * qb * dil + r, qb, stride=dil) if dil > 1 else pl.ds(qi * qb, qb)
            for t in range(n_tile):
                acc = jnp.where(low, pvs[u][t, 0], pvs[u][t, 1])
                m_new = jnp.where(low, stats[u][t, 0][0], stats[u][t, 1][0])
                l_new = jnp.where(low, stats[u][t, 0][1], stats[u][t, 1][1])
                if g > 0:
                    m_old = m_scr[t, tok, :]
                    m_tot = jnp.maximum(m_old, m_new)
                    e_old, e_new = jnp.exp(m_old - m_tot), jnp.exp(m_new - m_tot)
                    acc = acc_scr[t, tok, :] * e_old + acc * e_new
                    l_new = l_scr[t, tok, :] * e_old + l_new * e_new
                    m_new = m_tot
                acc_scr[t, tok, :] = acc
                m_scr[t, tok, :] = m_new
                l_scr[t, tok, :] = l_new
    for t in range(n_tile):
        o_ref[:, t * LANES:(t + 1) * LANES] = acc_scr[t] / l_scr[t]


def _attention(at, batch, seq):
    T = batch * seq
    nblk = seq // ATT_TOKENS
    args, specs = [], []
    for a, (_, d) in zip(at, DIL_PAIRS):
        rows = ATT_TOKENS // d
        n_prev = rows // ATT_BLOCK
        a3 = a.reshape(batch, seq // d, d * QKV_WIDTH)
        args += [a3, a3]
        specs += [pl.BlockSpec((1, rows, d * QKV_WIDTH), lambda b, i: (b, i, 0)),
                  pl.BlockSpec((1, ATT_BLOCK, d * QKV_WIDTH), lambda b, i, n=n_prev: (b, jnp.maximum(i * n - 1, 0), 0))]
    return pl.pallas_call(
        _attn_kernel,
        grid=(batch, nblk),
        in_specs=specs,
        out_specs=pl.BlockSpec((ATT_TOKENS, MIX_WIDTH), lambda b, i: (b * nblk + i, 0)),
        out_shape=jax.ShapeDtypeStruct((T, MIX_WIDTH), F32),
        scratch_shapes=[pltpu.VMEM((MIX_WIDTH // LANES, ATT_TOKENS, LANES), F32)] * 3,
        compiler_params=_params(2),
        name="dilated_attn",
    )(*args)


def _merge_kernel(x_ref, nw_ref, wg_ref, oa_ref, ob_ref, oc_ref, od_ref, wb_ref, wo_ref, out_ref):
    x = x_ref[...]
    hb = _rms(x, nw_ref[...]).astype(BF16)
    acc = jnp.zeros(x.shape, F32)
    for kbr, o in enumerate((oa_ref[...], ob_ref[...], oc_ref[...], od_ref[...])):
        gate = _sigmoid(jnp.dot(hb, wg_ref[:, kbr * D_MODEL:(kbr + 1) * D_MODEL], preferred_element_type=F32))
        acc = acc + gate * _mm(o, wb_ref[kbr])
    out_ref[...] = x + _mm(acc, wo_ref[...])


def _merge(x2, nw, w_gate, o_a, o_b, o_c, o_d, w_branch, w_out):
    T, D = x2.shape
    tm = ROW_TILE
    row = lambda w: pl.BlockSpec((tm, w), lambda i: (i, 0))
    return pl.pallas_call(
        _merge_kernel,
        grid=(T // tm,),
        in_specs=[row(D), _const_spec((1, D)), _const_spec(w_gate.shape)] + [row(MIX_WIDTH)] * N_BRANCH
                 + [_const_spec(w_branch.shape), _const_spec(w_out.shape)],
        out_specs=row(D),
        out_shape=jax.ShapeDtypeStruct((T, D), F32),
        compiler_params=_params(1),
        name="merge",
    )(x2, nw, w_gate, o_a, o_b, o_c, o_d, w_branch, w_out)


def _ffn_kernel(x_ref, nw_ref, wup_ref, cw_ref, cb_ref, wdn_ref, fw_ref, out_ref, tail_ref, *, n_s, final_norm):
    tm = x_ref.shape[0]

    @pl.when(pl.program_id(0) % n_s == 0)
    def _():
        tail_ref[...] = jnp.zeros_like(tail_ref)

    x = x_ref[...]
    hb = _rms(x, nw_ref[...]).astype(BF16)
    up = jnp.dot(hb, wup_ref[:, :D_FF], preferred_element_type=F32)
    gate = jnp.dot(hb, wup_ref[:, D_FF:], preferred_element_type=F32)
    row = lax.broadcasted_iota(jnp.int32, up.shape, 0)
    tail = tail_ref[...]
    up1 = jnp.where(row == 0, tail[1:2, :], pltpu.roll(up, 1, 0))
    up2 = jnp.where(row == 0, tail[0:1, :], jnp.where(row == 1, tail[1:2, :], pltpu.roll(up, 2, 0)))
    tail_ref[...] = up[tm - 2:tm, :]
    cw = cw_ref[...]
    conv = cb_ref[...] + cw[0:1, :] * up + cw[1:2, :] * up1 + cw[2:3, :] * up2
    act = conv * _sigmoid(conv) * gate
    y = x + _mm(act, wdn_ref[...])
    out_ref[...] = _rms(y, fw_ref[...]) if final_norm else y


def _ffn(x2, nw, w_up, conv_w, conv_b, w_down, final_w, seq, final_norm):
    T, D = x2.shape
    tm = ROW_TILE
    row = pl.BlockSpec((tm, D), lambda i: (i, 0))
    return pl.pallas_call(
        functools.partial(_ffn_kernel, n_s=seq // tm, final_norm=final_norm),
        grid=(T // tm,),
        in_specs=[row, _const_spec((1, D)), _const_spec(w_up.shape), _const_spec(conv_w.shape),
                  _const_spec((1, D_FF)), _const_spec(w_down.shape), _const_spec((1, D))],
        out_specs=row,
        out_shape=jax.ShapeDtypeStruct((T, D), F32),
        scratch_shapes=[pltpu.VMEM((CONV_W - 1, D_FF), F32)],
        compiler_params=_params(1),
        name="convglu",
    )(x2, nw, w_up, conv_w, conv_b, w_down, final_w)


def kernel(x, norm_mix_w, norm_ffn_w, norm_final_w, w_in, hg_lb_table, hg_norm_w, gla_a_w2, gla_a_b, gla_norm_w, rw_mu, rw_w0, rw_w2, rw_a0, rw_a2, rw_g2, rw_k_k, rw_k_a, rw_r_k, rw_ln_w, rw_ln_b, rw_v0, rw_v1, rw_v2, w_branch, w_out, ffn_w_up, ffn_conv_w, ffn_conv_b, ffn_w_down):
    B, S, D = x.shape
    T = B * S
    x2 = x.reshape(T, D)
    cos_t, sin_t = _rope_tables(S)
    row = lambda t: t.reshape(1, -1)
    W = MIX_WIDTH
    src = np.concatenate([np.arange(0, W), np.arange(W + RW_DECAY_LORA, 3 * W + RW_DECAY_LORA),
                          np.arange(W, W + RW_DECAY_LORA), np.arange(3 * W + RW_DECAY_LORA, RW_WIDTH)])
    rw_runs = [(0, W), (W + RW_DECAY_LORA, 3 * W + RW_DECAY_LORA), (W, W + RW_DECAY_LORA),
               (3 * W + RW_DECAY_LORA, RW_WIDTH)]
    at_runs = [(j * AT_WIDTH + g * W, j * AT_WIDTH + (g + 1) * W) for g in range(N_DIL) for j in range(3)]
    o_hg, o_gla, o_rw, o_at = 0, HG_WIDTH, HG_WIDTH + GLA_WIDTH, HG_WIDTH + GLA_WIDTH + RW_WIDTH
    o_gate = o_at + 3 * AT_WIDTH
    v_first = None
    for layer in range(DEPTH):
        w = w_in[layer].astype(BF16)
        cols = lambda lo, runs: jnp.concatenate([w[:, lo + a:lo + b] for a, b in runs], axis=1)
        w_hg = w[:, o_hg:o_gla]
        w_gla = w[:, o_gla:o_rw]
        w_rw = cols(o_rw, rw_runs)
        w_at = cols(o_at, at_runs)
        w_gate = w[:, o_gate:]
        z_hg, z_gla, z_rw, *at = _inproj(x2, row(norm_mix_w[layer]), cos_t, sin_t, w_hg, w_gla, w_rw, w_at, S)

        o_a = _hgrn2(z_hg, hg_lb_table, row(hg_norm_w[layer]), layer, S)
        o_b = _gla(z_gla, gla_a_w2[layer].astype(BF16), row(gla_a_b[layer]), row(gla_norm_w[layer]), S)
        p = {"mu": row(rw_mu[layer][src]), "w0": row(rw_w0[layer]), "w2": rw_w2[layer].astype(BF16),
             "a0": row(rw_a0[layer]), "a2": rw_a2[layer].astype(BF16), "g2": rw_g2[layer].astype(BF16),
             "k_k": row(rw_k_k[layer]), "k_a": row(rw_k_a[layer]), "r_k": row(rw_r_k[layer]),
             "ln_w": row(rw_ln_w[layer]), "ln_b": row(rw_ln_b[layer])}
        if layer > 0:
            p.update(v0=row(rw_v0[layer - 1]), v1=rw_v1[layer - 1].astype(BF16), v2=rw_v2[layer - 1].astype(BF16))
        o_c, v_new = _rwkv(z_rw, v_first, p, S)
        if layer == 0:
            v_first = v_new
        o_d = _attention(at, B, S)

        x2 = _merge(x2, row(norm_mix_w[layer]), w_gate, o_a, o_b, o_c, o_d,
                    w_branch[layer].astype(BF16), w_out[layer].astype(BF16))
        x2 = _ffn(x2, row(norm_ffn_w[layer]), ffn_w_up[layer].astype(BF16), ffn_conv_w[layer],
                  row(ffn_conv_b[layer]), ffn_w_down[layer].astype(BF16), row(norm_final_w), S,
                  final_norm=(layer == DEPTH - 1))
    return x2.reshape(B, S, D)
```

```python
import functools

import numpy as np
import jax
import jax.numpy as jnp
from jax import lax
from jax.experimental import pallas as pl
from jax.experimental.pallas import tpu as pltpu

F32 = jnp.float32
BF16 = jnp.bfloat16

D_MODEL = 1024
DEPTH = 2
HEAD_DIM = 64
MIX_WIDTH = D_MODEL // 4
N_BRANCH = 4
NORM_EPS = 1e-6
MASK_VALUE = -1e9
EXP_CLIP = 60.0
N_HEADS = MIX_WIDTH // HEAD_DIM
GLA_DK = 32
GLA_RANK = 16
GLA_TAU = 16.0
RW_DECAY_LORA = 32
RW_AAA_LORA = 32
RW_MV_LORA = 32
RW_GATE_LORA = 64
RW_GN_EPS = 64e-5
DIL_PAIRS = ((128, 1), (512, 4), (2048, 16))
N_DIL = 3
AT_WIDTH = N_DIL * N_HEADS * HEAD_DIM
ROPE_THETA = 10000.0
D_FF = 11 * D_MODEL // 4
CONV_W = 3

HG_WIDTH = 4 * MIX_WIDTH
GLA_QK = N_HEADS * GLA_DK
GLA_WIDTH = 2 * GLA_QK + 2 * MIX_WIDTH + GLA_RANK
RW_WIDTH = 3 * MIX_WIDTH + RW_DECAY_LORA + RW_AAA_LORA + RW_GATE_LORA
GATE_WIDTH = N_BRANCH * D_MODEL

LANES = 128
VMEM_LIMIT_BYTES = 56 * 1024 * 1024
ROW_TILE = 512
TIME_BLOCK = 512
GLA_CHUNK = 128
RW_CHUNK = 64
ATT_BLOCK = 128


def _const_spec(shape):
    nd = len(shape)
    return pl.BlockSpec(shape, lambda *_: (0,) * nd, pipeline_mode=pl.Buffered(1))


def _layer_spec(shape, layer):
    nd = len(shape)
    return pl.BlockSpec((None,) + tuple(shape[1:]), lambda *_: (layer,) + (0,) * (nd - 1), pipeline_mode=pl.Buffered(1))


def _params(n_grid):
    return pltpu.CompilerParams(dimension_semantics=("arbitrary",) * n_grid, vmem_limit_bytes=VMEM_LIMIT_BYTES)


def _mm(a, b):
    return jnp.dot(a.astype(BF16), b.astype(BF16), preferred_element_type=F32)


def _mm_nt(a, b):
    return lax.dot_general(a.astype(BF16), b.astype(BF16), (((1,), (1,)), ((), ())), preferred_element_type=F32)


def _mm_tn(a, b):
    return lax.dot_general(a.astype(BF16), b.astype(BF16), (((0,), (0,)), ((), ())), preferred_element_type=F32)


def _split2(a):
    hi = a.astype(BF16)
    lo = (a - hi.astype(F32)).astype(BF16)
    return hi, lo


def _mm_sel_lhs(sel, a):
    hi, lo = _split2(a)
    return jnp.dot(sel, hi, preferred_element_type=F32) + jnp.dot(sel, lo, preferred_element_type=F32)


def _mm_sel_rhs(a, sel):
    hi, lo = _split2(a)
    return jnp.dot(hi, sel, preferred_element_type=F32) + jnp.dot(lo, sel, preferred_element_type=F32)


def _rms(x, w):
    return x * lax.rsqrt(jnp.mean(x * x, axis=-1, keepdims=True) + NORM_EPS) * w


def _sigmoid(x):
    return jax.nn.sigmoid(x)


def _log_sigmoid(x):
    return jnp.minimum(x, 0.0) - jnp.log1p(jnp.exp(-jnp.abs(x)))


def _softplus(x):
    return jnp.maximum(x, 0.0) + jnp.log1p(jnp.exp(-jnp.abs(x)))


def _n_levels(chunk):
    return int(np.log2(chunk))


def _prefix_matrix(chunk):
    C, L = chunk, _n_levels(chunk)
    M = np.zeros(((L + 1) * C, C), np.float32)
    M[:C] = np.tril(np.ones((C, C), np.float32))
    for l in range(L):
        m = C >> (l + 1)
        for t in range(C):
            mid = (t // (2 * m)) * 2 * m + m - 1
            if t % (2 * m) >= m:
                M[(l + 1) * C + t, mid + 1:t + 1] = 1.0
            else:
                M[(l + 1) * C + t, t + 1:mid + 1] = 1.0
    return jnp.asarray(M, BF16)


def _block_ones(n_groups, w_in, w_out):
    return jnp.asarray(np.kron(np.eye(n_groups, dtype=np.float32), np.ones((w_in, w_out), np.float32)), BF16)


def _rope_tables(seq):
    half = HEAD_DIM // 2
    inv = ROPE_THETA ** (-np.arange(half, dtype=np.float64) / half)
    ang = np.arange(seq, dtype=np.float64)[:, None] * inv[None, :]
    cos, sin = np.cos(ang), np.sin(ang)
    cos_h = np.concatenate([cos, cos], axis=1)
    sin_h = np.concatenate([-sin, sin], axis=1)
    return (jnp.asarray(np.tile(cos_h, (1, N_HEADS)), F32), jnp.asarray(np.tile(sin_h, (1, N_HEADS)), F32))


QKV_WIDTH = 3 * MIX_WIDTH


def _inproj_kernel(x_ref, nw_ref, cos_ref, sin_ref, whg_ref, wgla_ref, wrw_ref, wat_ref,
                   zhg_ref, zgla_ref, zrw_ref, at1_ref, at4_ref, at16_ref, at_scr):
    tm = x_ref.shape[0]
    hb = _rms(x_ref[...], nw_ref[...]).astype(BF16)
    zhg_ref[...] = jnp.dot(hb, whg_ref[...], preferred_element_type=F32)
    zgla_ref[...] = jnp.dot(hb, wgla_ref[...], preferred_element_type=F32)
    zrw_ref[...] = jnp.dot(hb, wrw_ref[...], preferred_element_type=F32)
    zat = jnp.dot(hb, wat_ref[...], preferred_element_type=F32)
    cos, sin = cos_ref[...], sin_ref[...]
    lane = lax.broadcasted_iota(jnp.int32, cos.shape, 1)
    first_half = (lane % HEAD_DIM) < (HEAD_DIM // 2)

    def rope(p):
        partner = jnp.where(first_half, pltpu.roll(p, MIX_WIDTH - HEAD_DIM // 2, 1), pltpu.roll(p, HEAD_DIM // 2, 1))
        return p * cos + partner * sin

    n_lane_blk = QKV_WIDTH // LANES
    for g, (out_ref, (_, dil)) in enumerate(zip((at1_ref, at4_ref, at16_ref), DIL_PAIRS)):
        c0 = g * QKV_WIDTH
        q = rope(zat[:, c0:c0 + MIX_WIDTH]) * (HEAD_DIM ** -0.5)
        k = rope(zat[:, c0 + MIX_WIDTH:c0 + 2 * MIX_WIDTH])
        qkv = jnp.concatenate([q, k, zat[:, c0 + 2 * MIX_WIDTH:c0 + QKV_WIDTH]], axis=1)
        if dil == 1:
            out_ref[...] = qkv.astype(BF16)
            continue
        for c in range(n_lane_blk):
            at_scr[c] = qkv[:, c * LANES:(c + 1) * LANES]
        for r in range(dil):
            rows = jnp.concatenate([at_scr[c, pl.ds(r, tm // dil, stride=dil), :] for c in range(n_lane_blk)], axis=1)
            out_ref[:, r * QKV_WIDTH:(r + 1) * QKV_WIDTH] = rows.astype(BF16)


def _inproj(x2, nw, cos_t, sin_t, w_hg, w_gla, w_rw, w_at, layer, seq):
    T, D = x2.shape
    tm = ROW_TILE
    n_s = seq // tm
    row = lambda w: pl.BlockSpec((tm, w), lambda i: (i, 0))
    pos = pl.BlockSpec((tm, MIX_WIDTH), lambda i: (i % n_s, 0))
    widths = (HG_WIDTH, GLA_WIDTH, RW_WIDTH)
    dils = [d for _, d in DIL_PAIRS]
    return pl.pallas_call(
        _inproj_kernel,
        grid=(T // tm,),
        in_specs=[row(D), _const_spec((1, D)), pos, pos,
                  *[_layer_spec(w.shape, layer) for w in (w_hg, w_gla, w_rw, w_at)]],
        out_specs=[row(w) for w in widths]
                  + [pl.BlockSpec((tm // d, d * QKV_WIDTH), lambda i: (i, 0)) for d in dils],
        out_shape=[jax.ShapeDtypeStruct((T, w), F32) for w in widths]
                  + [jax.ShapeDtypeStruct((T // d, d * QKV_WIDTH), BF16) for d in dils],
        scratch_shapes=[pltpu.VMEM((QKV_WIDTH // LANES, tm, LANES), F32)],
        compiler_params=_params(1),
        name="inproj",
    )(x2, nw, cos_t, sin_t, w_hg, w_gla, w_rw, w_at)


def _level_masks(chunk):
    t = lax.broadcasted_iota(jnp.int32, (chunk, chunk), 0)
    s = lax.broadcasted_iota(jnp.int32, (chunk, chunk), 1)
    masks = []
    for l in range(_n_levels(chunk)):
        m = chunk >> (l + 1)
        sh = int(np.log2(2 * m))
        masks.append(((t >> sh) == (s >> sh)) & ((t & m) != 0) & ((s & m) == 0))
    return masks


def _gla_block(q, k, v, g, st_ref, pm_ref, qk_ones, *, chunk, dk, dv):
    C = chunk
    nc = q.shape[0] // C
    n_lvl = _n_levels(C)
    masks = _level_masks(C)
    pm = pm_ref[...]
    rows = lambda t, c: t[c * C:(c + 1) * C]
    stack = lambda f: jnp.concatenate([f(c) for c in range(nc)], axis=0)
    D = [_mm_sel_lhs(pm, rows(g, c)) for c in range(nc)]
    b = stack(lambda c: D[c][:C])
    b_last = stack(lambda c: jnp.broadcast_to(D[c][C - 1:C, :], (C, b.shape[1])))
    dec_last = [jnp.exp(D[c][C - 1:C, :]) for c in range(nc)]
    q_in = (q * jnp.exp(b)).astype(BF16)
    k_dec = (k * jnp.exp(b_last - b)).astype(BF16)
    vb = v.astype(BF16)
    qf, kf = [], []
    for l in range(n_lvl):
        e = jnp.exp(stack(lambda c: D[c][(l + 1) * C:(l + 2) * C]))
        qf.append((q * e).astype(BF16))
        kf.append((k * e).astype(BF16))
    diag = _mm(q * k, qk_ones)
    pairs = [(c, h) for c in range(nc) for h in range(N_HEADS)]
    bk = lambda t, c, h: t[c * C:(c + 1) * C, h * dk:(h + 1) * dk]
    bv = lambda t, c, h: t[c * C:(c + 1) * C, h * dv:(h + 1) * dv]
    nt = (((1,), (1,)), ((), ()))
    tn = (((0,), (0,)), ((), ()))
    prod = {(p, l): lax.dot_general(bk(qf[l], *p), bk(kf[l], *p), nt, preferred_element_type=F32)
            for p in pairs for l in range(n_lvl)}
    o_intra, kv = {}, {}
    for p in pairs:
        sc = jnp.zeros((C, C), F32)
        for l in range(n_lvl):
            sc = jnp.where(masks[l], prod[p, l], sc)
        o_intra[p] = jnp.dot(sc.astype(BF16), bv(vb, *p), preferred_element_type=F32)
        kv[p] = lax.dot_general(bv(vb, *p), bk(k_dec, *p), tn, preferred_element_type=F32)
    s = [st_ref[h] for h in range(N_HEADS)]
    out_rows = []
    for c in range(nc):
        o_inter = [lax.dot_general(bk(q_in, c, h), s[h].astype(BF16), nt, preferred_element_type=F32)
                   for h in range(N_HEADS)]
        s = [s[h] * dec_last[c][:, h * dk:(h + 1) * dk] + kv[c, h] for h in range(N_HEADS)]
        out_rows.append(jnp.concatenate([o_intra[c, h] + o_inter[h] for h in range(N_HEADS)], axis=1))
    for h in range(N_HEADS):
        st_ref[h] = s[h]
    return jnp.concatenate(out_rows, axis=0) + diag * v


def _head_rmsnorm(o, head_ones, w):
    ms = _mm_sel_rhs(o * o, head_ones) * (1.0 / HEAD_DIM)
    return o * lax.rsqrt(ms + NORM_EPS) * w


def _hgrn2_kernel(z_ref, lbt_ref, nw_ref, pm_ref, ones_ref, o_ref, st_ref, *, layer):
    @pl.when(pl.program_id(1) == 0)
    def _():
        st_ref[...] = jnp.zeros_like(st_ref)

    tab = lbt_ref[...]
    e = jnp.exp(tab - jnp.max(tab, axis=0, keepdims=True))
    p = e / jnp.sum(e, axis=0, keepdims=True)
    lb = jnp.zeros((1, MIX_WIDTH), F32)
    for i in range(1, layer + 1):
        lb = lb + p[i:i + 1, :]
    ones = ones_ref[...]
    z = z_ref[...]
    q_raw, fz = z[:, :MIX_WIDTH], z[:, MIX_WIDTH:2 * MIX_WIDTH]
    i_raw, g_raw = z[:, 2 * MIX_WIDTH:3 * MIX_WIDTH], z[:, 3 * MIX_WIDTH:]
    log_f = _log_sigmoid(fz) + jnp.log1p(lb * jnp.exp(jnp.minimum(-fz, EXP_CLIP)))
    k = (1.0 - lb) * _sigmoid(-fz)
    q = q_raw * _sigmoid(q_raw)
    o = _gla_block(q, k, i_raw, log_f, st_ref, pm_ref, ones, chunk=GLA_CHUNK, dk=HEAD_DIM, dv=HEAD_DIM)
    o_ref[...] = _head_rmsnorm(o, ones, nw_ref[...]) * _sigmoid(g_raw)


def _gla_kernel(z_ref, aw2_ref, ab_ref, nw_ref, pm_ref, qk_ones_ref, ones_ref, o_ref, st_ref):
    @pl.when(pl.program_id(1) == 0)
    def _():
        st_ref[...] = jnp.zeros_like(st_ref)

    z = z_ref[...]
    q = z[:, :GLA_QK] * (GLA_DK ** -0.5)
    k = z[:, GLA_QK:2 * GLA_QK]
    v = z[:, 2 * GLA_QK:2 * GLA_QK + MIX_WIDTH]
    gate = z[:, 2 * GLA_QK + MIX_WIDTH:2 * GLA_QK + 2 * MIX_WIDTH]
    a_low = z[:, 2 * GLA_QK + 2 * MIX_WIDTH:]
    log_alpha = _log_sigmoid(_mm(a_low, aw2_ref[...]) + ab_ref[...]) * (1.0 / GLA_TAU)
    o = _gla_block(q, k, v, log_alpha, st_ref, pm_ref, qk_ones_ref[...], chunk=GLA_CHUNK, dk=GLA_DK, dv=HEAD_DIM)
    o_ref[...] = _head_rmsnorm(o, ones_ref[...], nw_ref[...]) * (gate * _sigmoid(gate))


def _time_grid(T, seq):
    ts = TIME_BLOCK
    n_t = seq // ts
    return ts, n_t, (T // seq, n_t)


def _hgrn2(z_hg, lb_table, norm_w, layer, seq):
    T = z_hg.shape[0]
    ts, n_t, grid = _time_grid(T, seq)
    blk = lambda w: pl.BlockSpec((ts, w), lambda b, i: (b * n_t + i, 0))
    pm = _prefix_matrix(GLA_CHUNK)
    ones = _block_ones(N_HEADS, HEAD_DIM, HEAD_DIM)
    return pl.pallas_call(
        functools.partial(_hgrn2_kernel, layer=layer),
        grid=grid,
        in_specs=[blk(HG_WIDTH), _const_spec(lb_table.shape), _const_spec((1, MIX_WIDTH)),
                  _const_spec(pm.shape), _const_spec(ones.shape)],
        out_specs=blk(MIX_WIDTH),
        out_shape=jax.ShapeDtypeStruct((T, MIX_WIDTH), F32),
        scratch_shapes=[pltpu.VMEM((N_HEADS, HEAD_DIM, HEAD_DIM), F32)],
        compiler_params=_params(2),
        name="hgrn2",
    )(z_hg, lb_table, norm_w, pm, ones)


def _gla(z_gla, a_w2, a_b, norm_w, seq):
    T = z_gla.shape[0]
    ts, n_t, grid = _time_grid(T, seq)
    blk = lambda w: pl.BlockSpec((ts, w), lambda b, i: (b * n_t + i, 0))
    pm = _prefix_matrix(GLA_CHUNK)
    qk_ones = _block_ones(N_HEADS, GLA_DK, HEAD_DIM)
    ones = _block_ones(N_HEADS, HEAD_DIM, HEAD_DIM)
    return pl.pallas_call(
        _gla_kernel,
        grid=grid,
        in_specs=[blk(GLA_WIDTH), _const_spec(a_w2.shape), _const_spec((1, GLA_QK)), _const_spec((1, MIX_WIDTH)),
                  _const_spec(pm.shape), _const_spec(qk_ones.shape), _const_spec(ones.shape)],
        out_specs=blk(MIX_WIDTH),
        out_shape=jax.ShapeDtypeStruct((T, MIX_WIDTH), F32),
        scratch_shapes=[pltpu.VMEM((N_HEADS, HEAD_DIM, GLA_DK), F32)],
        compiler_params=_params(2),
        name="gla",
    )(z_gla, a_w2, a_b, norm_w, pm, qk_ones, ones)


_RW_R, _RW_K, _RW_V = 0, MIX_WIDTH, 2 * MIX_WIDTH
_RW_WL = 3 * MIX_WIDTH
_RW_AL = _RW_WL + RW_DECAY_LORA
_RW_GL = _RW_AL + RW_AAA_LORA


def _rwkv_kernel(*refs, has_vfirst):
    if has_vfirst:
        (z_ref, vf_ref, mu_ref, w0_ref, w2_ref, a0_ref, a2_ref, g2_ref, kk_ref, ka_ref, rk_ref, lnw_ref, lnb_ref,
         v0_ref, v1_ref, v2_ref, tri_ref, ones_ref, o_ref, s_ref, carry_ref) = refs
    else:
        (z_ref, mu_ref, w0_ref, w2_ref, a0_ref, a2_ref, g2_ref, kk_ref, ka_ref, rk_ref, lnw_ref, lnb_ref,
         tri_ref, ones_ref, o_ref, vout_ref, s_ref, carry_ref) = refs
    C, HD = RW_CHUNK, HEAD_DIM
    ts = z_ref.shape[0]
    nc = ts // C
    pairs = [(c, h) for c in range(nc) for h in range(N_HEADS)]

    @pl.when(pl.program_id(1) == 0)
    def _():
        s_ref[...] = jnp.zeros_like(s_ref)
        carry_ref[...] = jnp.zeros_like(carry_ref)

    z = z_ref[...]
    zp = pltpu.roll(z, 1, 0)
    zp = jnp.where(lax.broadcasted_iota(jnp.int32, z.shape, 0) == 0, carry_ref[...], zp)
    carry_ref[...] = z[ts - 1:ts, :]
    zs = z + mu_ref[...] * (zp - z)
    r, k, v = zs[:, _RW_R:_RW_K], zs[:, _RW_K:_RW_V], zs[:, _RW_V:_RW_WL]
    wl, al, gl = zs[:, _RW_WL:_RW_AL], zs[:, _RW_AL:_RW_GL], zs[:, _RW_GL:]
    ones = ones_ref[...]
    gsum = lambda t: _mm_sel_rhs(t, ones)
    w = -_softplus(-(w0_ref[...] + _mm(jnp.tanh(wl), w2_ref[...]))) - 0.5
    lw = -jnp.exp(w)
    a = _sigmoid(a0_ref[...] + _mm(al, a2_ref[...]))
    g = _mm(_sigmoid(gl), g2_ref[...])
    if has_vfirst:
        v = v + (vf_ref[...] - v) * _sigmoid(v0_ref[...] + _mm(_mm(v, v1_ref[...]), v2_ref[...]))
    else:
        vout_ref[...] = v
    kk = k * kk_ref[...]
    kk = kk / jnp.maximum(jnp.sqrt(gsum(kk * kk)), 1e-12)
    k_mod = k * (1.0 + (a - 1.0) * ka_ref[...])
    kka = kk * a

    tri = tri_ref[...]
    cums = [_mm_sel_lhs(tri, lw[c * C:(c + 1) * C]) for c in range(nc)]
    cum = jnp.concatenate(cums, axis=0)
    cum_last = jnp.concatenate([jnp.broadcast_to(cc[C - 1:C, :], (C, MIX_WIDTH)) for cc in cums], axis=0)
    w_end = [jnp.exp(cc[C - 1:C, :]) for cc in cums]
    r_t = r * jnp.exp(cum)
    a_t = -kk * jnp.exp(cum - lw)
    inv = jnp.exp(-cum)
    end = jnp.exp(cum_last - cum)
    r_tb, a_tb = r_t.astype(BF16), a_t.astype(BF16)
    k_hb, b_hb = (k_mod * inv).astype(BF16), (kka * inv).astype(BF16)
    k_eb, b_eb = (k_mod * end).astype(BF16), (kka * end).astype(BF16)
    vb = v.astype(BF16)
    blk = lambda t, c, h: t[c * C:(c + 1) * C, h * HD:(h + 1) * HD]

    ti = lax.broadcasted_iota(jnp.int32, (C, C), 0)
    si = lax.broadcasted_iota(jnp.int32, (C, C), 1)
    incl, strict = si <= ti, si < ti
    dot = lambda x, y: jnp.dot(x, y, preferred_element_type=F32)
    aa = {p: _mm_nt(jnp.concatenate([blk(r_tb, *p), blk(a_tb, *p)], axis=0),
                    jnp.concatenate([blk(k_hb, *p), blk(b_hb, *p)], axis=0)) for p in pairs}
    incl2 = (lax.broadcasted_iota(jnp.int32, (C, 2 * C), 1) & (C - 1)) <= lax.broadcasted_iota(jnp.int32, (C, 2 * C), 0)
    a_r = {p: jnp.where(incl2, aa[p][:C], 0.0).astype(BF16) for p in pairs}
    a_ak = {p: jnp.where(strict, aa[p][C:, :C], 0.0).astype(BF16) for p in pairs}
    npow = {p: jnp.where(strict, aa[p][C:, C:], 0.0) for p in pairs}
    pv = {p: dot(a_ak[p], blk(vb, *p)) for p in pairs}
    x = {p: jnp.concatenate([blk(a_t, *p), pv[p]], axis=1) for p in pairs}
    n_stage = _n_levels(C)
    for j in range(n_stage):
        last = j == n_stage - 1
        nb = {p: npow[p].astype(BF16) for p in pairs}
        nx = {p: dot(nb[p], x[p].astype(BF16)) for p in pairs}
        if not last:
            npow = {p: dot(nb[p], nb[p]) for p in pairs}
        x = {p: x[p] + nx[p] for p in pairs}
    rmat = {p: jnp.concatenate([jnp.concatenate([jnp.zeros((C, HD), BF16), blk(vb, *p)], axis=1), x[p].astype(BF16)],
                               axis=0) for p in pairs}
    qz = {p: dot(a_r[p], rmat[p]) for p in pairs}
    gh = {p: lax.dot_general(rmat[p], jnp.concatenate([blk(k_eb, *p), blk(b_eb, *p)], axis=0),
                             (((0,), (0,)), ((), ())), preferred_element_type=F32) for p in pairs}
    q_f = {p: (blk(r_t, *p) + qz[p][:, :HD]).astype(BF16) for p in pairs}

    s = [s_ref[h] for h in range(N_HEADS)]
    y_rows = []
    for c in range(nc):
        sb = [s[h].astype(BF16) for h in range(N_HEADS)]
        ys = [lax.dot_general(q_f[c, h], sb[h], (((1,), (1,)), ((), ())), preferred_element_type=F32)
              + qz[c, h][:, HD:] for h in range(N_HEADS)]
        sg = [dot(sb[h], gh[c, h][:HD].astype(BF16)) for h in range(N_HEADS)]
        s = [s[h] * w_end[c][:, h * HD:(h + 1) * HD] + sg[h] + gh[c, h][HD:] for h in range(N_HEADS)]
        y_rows.append(jnp.concatenate(ys, axis=1))
    for h in range(N_HEADS):
        s_ref[h] = s[h]
    y = jnp.concatenate(y_rows, axis=0)

    mean = gsum(y) * (1.0 / HD)
    yc = y - mean
    var = gsum(yc * yc) * (1.0 / HD)
    yn = yc * lax.rsqrt(var + RW_GN_EPS) * lnw_ref[...] + lnb_ref[...]
    bonus = gsum(r * k_mod * rk_ref[...]) * v
    o_ref[...] = (yn + bonus) * g


def _rwkv(z_rw, v_first, p, seq):
    T = z_rw.shape[0]
    ts, n_t, grid = _time_grid(T, seq)
    blk = lambda w: pl.BlockSpec((ts, w), lambda b, i: (b * n_t + i, 0))
    tri = jnp.asarray(np.tril(np.ones((RW_CHUNK, RW_CHUNK), np.float32)), BF16)
    ones = _block_ones(N_HEADS, HEAD_DIM, HEAD_DIM)
    has_vfirst = v_first is not None
    names = ["mu", "w0", "w2", "a0", "a2", "g2", "k_k", "k_a", "r_k", "ln_w", "ln_b"]
    args = [z_rw] + ([v_first] if has_vfirst else []) + [p[n] for n in names]
    specs = [blk(RW_WIDTH)] + ([blk(MIX_WIDTH)] if has_vfirst else []) + [_const_spec(p[n].shape) for n in names]
    if has_vfirst:
        args += [p["v0"], p["v1"], p["v2"]]
        specs += [_const_spec(p[n].shape) for n in ("v0", "v1", "v2")]
    args += [tri, ones]
    specs += [_const_spec(tri.shape), _const_spec(ones.shape)]
    out_sds = jax.ShapeDtypeStruct((T, MIX_WIDTH), F32)
    res = pl.pallas_call(
        functools.partial(_rwkv_kernel, has_vfirst=has_vfirst),
        grid=grid,
        in_specs=specs,
        out_specs=blk(MIX_WIDTH) if has_vfirst else [blk(MIX_WIDTH), blk(MIX_WIDTH)],
        out_shape=out_sds if has_vfirst else [out_sds, out_sds],
        scratch_shapes=[pltpu.VMEM((N_HEADS, HEAD_DIM, HEAD_DIM), F32), pltpu.VMEM((1, RW_WIDTH), F32)],
        compiler_params=_params(2),
        name="rwkv7",
    )(*args)
    return (res, None) if has_vfirst else (res[0], res[1])


ATT_TOKENS = ATT_BLOCK * max(d for _, d in DIL_PAIRS)
ATT_UNITS_PER_BATCH = 8


def _attn_kernel(c1_ref, p1_ref, c4_ref, p4_ref, c16_ref, p16_ref, o_ref, acc_scr, m_scr, l_scr):
    qb = ATT_BLOCK
    n_tile = MIX_WIDTH // LANES
    row = lax.broadcasted_iota(jnp.int32, (qb, 2 * qb), 0)
    col = lax.broadcasted_iota(jnp.int32, (qb, 2 * qb), 1)
    band = (col >= row) & (col <= row + qb)
    band_first = band & (col >= jnp.where(pl.program_id(1) > 0, 0, qb))
    low = lax.broadcasted_iota(jnp.int32, (qb, LANES), 1) < HEAD_DIM
    nt = (((1,), (1,)), ((), ()))
    curs, prevs = (c1_ref, c4_ref, c16_ref), (p1_ref, p4_ref, p16_ref)
    units = [(g, dil, r, qi) for g, (_, dil) in enumerate(DIL_PAIRS) for r in range(dil)
             for qi in range(ATT_TOKENS // dil // qb)]
    pairs = [(t, e) for t in range(n_tile) for e in range(LANES // HEAD_DIM)]
    tile = lambda x, t: x[:, t * LANES:(t + 1) * LANES]
    for b0 in range(0, len(units), ATT_UNITS_PER_BATCH):
        batch = units[b0:b0 + ATT_UNITS_PER_BATCH]
        vs, masks, scores = [], [], []
        for g, dil, r, qi in batch:
            cur, prev = curs[g], prevs[g]
            qc, kc, vc = (r * QKV_WIDTH + j * MIX_WIDTH for j in range(3))
            q = cur[0, qi * qb:(qi + 1) * qb, qc:qc + MIX_WIDTH]
            if qi == 0:
                k = jnp.concatenate([prev[0, :, kc:kc + MIX_WIDTH], cur[0, :qb, kc:kc + MIX_WIDTH]], axis=0)
                v = jnp.concatenate([prev[0, :, vc:vc + MIX_WIDTH], cur[0, :qb, vc:vc + MIX_WIDTH]], axis=0)
            else:
                k = cur[0, (qi - 1) * qb:(qi + 1) * qb, kc:kc + MIX_WIDTH]
                v = cur[0, (qi - 1) * qb:(qi + 1) * qb, vc:vc + MIX_WIDTH]
            vs.append(v)
            masks.append(band_first if qi == 0 else band)
            zero = jnp.zeros((qb, LANES), q.dtype)
            scores.append({(t, e): lax.dot_general(jnp.where(low if e == 0 else ~low, tile(q, t), zero), tile(k, t), nt,
                                                   preferred_element_type=F32) for t, e in pairs})
        probs, stats = [], []
        for u in range(len(batch)):
            pu, su = {}, {}
            for p in pairs:
                s = jnp.where(masks[u], scores[u][p], MASK_VALUE)
                m = jnp.max(s, axis=-1, keepdims=True)
                ex = jnp.exp(s - m)
                su[p] = (m, jnp.sum(ex, axis=-1, keepdims=True))
                pu[p] = ex.astype(BF16)
            probs.append(pu)
            stats.append(su)
        pvs = [{(t, e): jnp.dot(probs[u][t, e], tile(vs[u], t), preferred_element_type=F32) for t, e in pairs}
               for u in range(len(batch))]
        for u, (g, dil, r, qi) in enumerate(batch):
            tok = pl.ds(qi * qb * dil + r, qb, stride=dil) if dil > 1 else pl.ds(qi * qb, qb)
            for t in range(n_tile):
                acc = jnp.where(low, pvs[u][t, 0], pvs[u][t, 1])
                m_new = jnp.where(low, stats[u][t, 0][0], stats[u][t, 1][0])
                l_new = jnp.where(low, stats[u][t, 0][1], stats[u][t, 1][1])
                if g > 0:
                    m_old = m_scr[t, tok, :]
                    m_tot = jnp.maximum(m_old, m_new)
                    e_old, e_new = jnp.exp(m_old - m_tot), jnp.exp(m_new - m_tot)
                    acc = acc_scr[t, tok, :] * e_old + acc * e_new
                    l_new = l_scr[t, tok, :] * e_old + l_new * e_new
                    m_new = m_tot
                acc_scr[t, tok, :] = acc
                m_scr[t, tok, :] = m_new
                l_scr[t, tok, :] = l_new
    for t in range(n_tile):
        o_ref[:, t * LANES:(t + 1) * LANES] = acc_scr[t] / l_scr[t]


def _attention(at, batch, seq):
    T = batch * seq
    nblk = seq // ATT_TOKENS
    args, specs = [], []
    for a, (_, d) in zip(at, DIL_PAIRS):
        rows = ATT_TOKENS // d
        n_prev = rows // ATT_BLOCK
        a3 = a.reshape(batch, seq // d, d * QKV_WIDTH)
        args += [a3, a3]
        specs += [pl.BlockSpec((1, rows, d * QKV_WIDTH), lambda b, i: (b, i, 0)),
                  pl.BlockSpec((1, ATT_BLOCK, d * QKV_WIDTH), lambda b, i, n=n_prev: (b, jnp.maximum(i * n - 1, 0), 0))]
    return pl.pallas_call(
        _attn_kernel,
        grid=(batch, nblk),
        in_specs=specs,
        out_specs=pl.BlockSpec((ATT_TOKENS, MIX_WIDTH), lambda b, i: (b * nblk + i, 0)),
        out_shape=jax.ShapeDtypeStruct((T, MIX_WIDTH), F32),
        scratch_shapes=[pltpu.VMEM((MIX_WIDTH // LANES, ATT_TOKENS, LANES), F32)] * 3,
        compiler_params=_params(2),
        name="dilated_attn",
    )(*args)


def _merge_kernel(x_ref, nw_ref, wg_ref, oa_ref, ob_ref, oc_ref, od_ref, wb_ref, wo_ref, out_ref):
    x = x_ref[...]
    hb = _rms(x, nw_ref[...]).astype(BF16)
    acc = jnp.zeros(x.shape, F32)
    for kbr, o in enumerate((oa_ref[...], ob_ref[...], oc_ref[...], od_ref[...])):
        gate = _sigmoid(jnp.dot(hb, wg_ref[:, kbr * D_MODEL:(kbr + 1) * D_MODEL], preferred_element_type=F32))
        acc = acc + gate * _mm(o, wb_ref[kbr])
    out_ref[...] = x + _mm(acc, wo_ref[...])


def _merge(x2, nw, w_gate, o_a, o_b, o_c, o_d, w_branch, w_out, layer):
    T, D = x2.shape
    tm = ROW_TILE
    row = lambda w: pl.BlockSpec((tm, w), lambda i: (i, 0))
    return pl.pallas_call(
        _merge_kernel,
        grid=(T // tm,),
        in_specs=[row(D), _const_spec((1, D)), _layer_spec(w_gate.shape, layer)] + [row(MIX_WIDTH)] * N_BRANCH
                 + [_layer_spec(w_branch.shape, layer), _layer_spec(w_out.shape, layer)],
        out_specs=row(D),
        out_shape=jax.ShapeDtypeStruct((T, D), F32),
        compiler_params=_params(1),
        name="merge",
    )(x2, nw, w_gate, o_a, o_b, o_c, o_d, w_branch, w_out)


def _ffn_kernel(x_ref, nw_ref, wup_ref, cw_ref, cb_ref, wdn_ref, fw_ref, out_ref, tail_ref, *, n_s, final_norm):
    tm = x_ref.shape[0]

    @pl.when(pl.program_id(0) % n_s == 0)
    def _():
        tail_ref[...] = jnp.zeros_like(tail_ref)

    x = x_ref[...]
    hb = _rms(x, nw_ref[...]).astype(BF16)
    up = jnp.dot(hb, wup_ref[:, :D_FF], preferred_element_type=F32)
    gate = jnp.dot(hb, wup_ref[:, D_FF:], preferred_element_type=F32)
    row = lax.broadcasted_iota(jnp.int32, up.shape, 0)
    tail = tail_ref[...]
    up1 = jnp.where(row == 0, tail[1:2, :], pltpu.roll(up, 1, 0))
    up2 = jnp.where(row == 0, tail[0:1, :], jnp.where(row == 1, tail[1:2, :], pltpu.roll(up, 2, 0)))
    tail_ref[...] = up[tm - 2:tm, :]
    cw = cw_ref[...]
    conv = cb_ref[...] + cw[0:1, :] * up + cw[1:2, :] * up1 + cw[2:3, :] * up2
    act = conv * _sigmoid(conv) * gate
    y = x + _mm(act, wdn_ref[...])
    out_ref[...] = _rms(y, fw_ref[...]) if final_norm else y


def _ffn(x2, nw, w_up, conv_w, conv_b, w_down, final_w, layer, seq, final_norm):
    T, D = x2.shape
    tm = ROW_TILE
    row = pl.BlockSpec((tm, D), lambda i: (i, 0))
    return pl.pallas_call(
        functools.partial(_ffn_kernel, n_s=seq // tm, final_norm=final_norm),
        grid=(T // tm,),
        in_specs=[row, _const_spec((1, D)), _layer_spec(w_up.shape, layer), _const_spec(conv_w.shape),
                  _const_spec((1, D_FF)), _layer_spec(w_down.shape, layer), _const_spec((1, D))],
        out_specs=row,
        out_shape=jax.ShapeDtypeStruct((T, D), F32),
        scratch_shapes=[pltpu.VMEM((CONV_W - 1, D_FF), F32)],
        compiler_params=_params(1),
        name="convglu",
    )(x2, nw, w_up, conv_w, conv_b, w_down, final_w)


def kernel(x, norm_mix_w, norm_ffn_w, norm_final_w, w_in, hg_lb_table, hg_norm_w, gla_a_w2, gla_a_b, gla_norm_w, rw_mu, rw_w0, rw_w2, rw_a0, rw_a2, rw_g2, rw_k_k, rw_k_a, rw_r_k, rw_ln_w, rw_ln_b, rw_v0, rw_v1, rw_v2, w_branch, w_out, ffn_w_up, ffn_conv_w, ffn_conv_b, ffn_w_down):
    B, S, D = x.shape
    T = B * S
    x2 = x.reshape(T, D)
    cos_t, sin_t = _rope_tables(S)
    row = lambda t: t.reshape(1, -1)
    W = MIX_WIDTH
    src = np.concatenate([np.arange(0, W), np.arange(W + RW_DECAY_LORA, 3 * W + RW_DECAY_LORA),
                          np.arange(W, W + RW_DECAY_LORA), np.arange(3 * W + RW_DECAY_LORA, RW_WIDTH)])
    rw_runs = [(0, W), (W + RW_DECAY_LORA, 3 * W + RW_DECAY_LORA), (W, W + RW_DECAY_LORA),
               (3 * W + RW_DECAY_LORA, RW_WIDTH)]
    at_runs = [(j * AT_WIDTH + g * W, j * AT_WIDTH + (g + 1) * W) for g in range(N_DIL) for j in range(3)]
    o_hg, o_gla, o_rw, o_at = 0, HG_WIDTH, HG_WIDTH + GLA_WIDTH, HG_WIDTH + GLA_WIDTH + RW_WIDTH
    o_gate = o_at + 3 * AT_WIDTH
    w = w_in.astype(BF16)
    cols = lambda lo, runs: jnp.concatenate([w[:, :, lo + a:lo + b] for a, b in runs], axis=2)
    w_hg, w_gla, w_gate = w[:, :, o_hg:o_gla], w[:, :, o_gla:o_rw], w[:, :, o_gate:]
    w_rw, w_at = cols(o_rw, rw_runs), cols(o_at, at_runs)
    w_branch_b, w_out_b = w_branch.astype(BF16), w_out.astype(BF16)
    w_up_b, w_down_b = ffn_w_up.astype(BF16), ffn_w_down.astype(BF16)
    v_first = None
    for layer in range(DEPTH):
        z_hg, z_gla, z_rw, *at = _inproj(x2, row(norm_mix_w[layer]), cos_t, sin_t, w_hg, w_gla, w_rw, w_at, layer, S)

        o_a = _hgrn2(z_hg, hg_lb_table, row(hg_norm_w[layer]), layer, S)
        o_b = _gla(z_gla, gla_a_w2[layer].astype(BF16), row(gla_a_b[layer]), row(gla_norm_w[layer]), S)
        p = {"mu": row(rw_mu[layer][src]), "w0": row(rw_w0[layer]), "w2": rw_w2[layer].astype(BF16),
             "a0": row(rw_a0[layer]), "a2": rw_a2[layer].astype(BF16), "g2": rw_g2[layer].astype(BF16),
             "k_k": row(rw_k_k[layer]), "k_a": row(rw_k_a[layer]), "r_k": row(rw_r_k[layer]),
             "ln_w": row(rw_ln_w[layer]), "ln_b": row(rw_ln_b[layer])}
        if layer > 0:
            p.update(v0=row(rw_v0[layer - 1]), v1=rw_v1[layer - 1].astype(BF16), v2=rw_v2[layer - 1].astype(BF16))
        o_c, v_new = _rwkv(z_rw, v_first, p, S)
        if layer == 0:
            v_first = v_new
        o_d = _attention(at, B, S)

        x2 = _merge(x2, row(norm_mix_w[layer]), w_gate, o_a, o_b, o_c, o_d, w_branch_b, w_out_b, layer)
        x2 = _ffn(x2, row(norm_ffn_w[layer]), w_up_b, ffn_conv_w[layer],
                  row(ffn_conv_b[layer]), w_down_b, row(norm_final_w), layer, S,
                  final_norm=(layer == DEPTH - 1))
    return x2.reshape(B, S, D)
```

```python
import functools

import numpy as np
import jax
import jax.numpy as jnp
from jax import lax
from jax.experimental import pallas as pl
from jax.experimental.pallas import tpu as pltpu

F32 = jnp.float32
BF16 = jnp.bfloat16

D_MODEL = 1024
DEPTH = 2
HEAD_DIM = 64
MIX_WIDTH = D_MODEL // 4
N_BRANCH = 4
NORM_EPS = 1e-6
MASK_VALUE = -1e9
EXP_CLIP = 60.0
N_HEADS = MIX_WIDTH // HEAD_DIM
GLA_DK = 32
GLA_RANK = 16
GLA_TAU = 16.0
RW_DECAY_LORA = 32
RW_AAA_LORA = 32
RW_MV_LORA = 32
RW_GATE_LORA = 64
RW_GN_EPS = 64e-5
DIL_PAIRS = ((128, 1), (512, 4), (2048, 16))
N_DIL = 3
AT_WIDTH = N_DIL * N_HEADS * HEAD_DIM
ROPE_THETA = 10000.0
D_FF = 11 * D_MODEL // 4
CONV_W = 3

HG_WIDTH = 4 * MIX_WIDTH
GLA_QK = N_HEADS * GLA_DK
GLA_WIDTH = 2 * GLA_QK + 2 * MIX_WIDTH + GLA_RANK
RW_WIDTH = 3 * MIX_WIDTH + RW_DECAY_LORA + RW_AAA_LORA + RW_GATE_LORA
GATE_WIDTH = N_BRANCH * D_MODEL

LANES = 128
LOG2_E = 1.4426950408889634
VMEM_LIMIT_BYTES = 56 * 1024 * 1024
ROW_TILE = 512
TIME_BLOCK = 512
GLA_CHUNK = 128
RW_CHUNK = 64
ATT_BLOCK = 128


def _const_spec(shape):
    nd = len(shape)
    return pl.BlockSpec(shape, lambda *_: (0,) * nd, pipeline_mode=pl.Buffered(1))


def _layer_spec(shape, layer):
    nd = len(shape)
    return pl.BlockSpec((None,) + tuple(shape[1:]), lambda *_: (layer,) + (0,) * (nd - 1), pipeline_mode=pl.Buffered(1))


def _params(n_grid):
    return pltpu.CompilerParams(dimension_semantics=("arbitrary",) * n_grid, vmem_limit_bytes=VMEM_LIMIT_BYTES)


def _mm(a, b):
    return jnp.dot(a.astype(BF16), b.astype(BF16), preferred_element_type=F32)


def _mm_nt(a, b):
    return lax.dot_general(a.astype(BF16), b.astype(BF16), (((1,), (1,)), ((), ())), preferred_element_type=F32)


def _mm_tn(a, b):
    return lax.dot_general(a.astype(BF16), b.astype(BF16), (((0,), (0,)), ((), ())), preferred_element_type=F32)


def _split2(a):
    hi = a.astype(BF16)
    lo = (a - hi.astype(F32)).astype(BF16)
    return hi, lo


def _mm_sel_lhs(sel, a):
    hi, lo = _split2(a)
    return jnp.dot(sel, hi, preferred_element_type=F32) + jnp.dot(sel, lo, preferred_element_type=F32)


def _mm_sel_rhs(a, sel):
    hi, lo = _split2(a)
    return jnp.dot(hi, sel, preferred_element_type=F32) + jnp.dot(lo, sel, preferred_element_type=F32)


def _rms(x, w):
    return x * lax.rsqrt(jnp.mean(x * x, axis=-1, keepdims=True) + NORM_EPS) * w


def _sigmoid(x):
    return jax.nn.sigmoid(x)


def _log_sigmoid(x):
    return jnp.minimum(x, 0.0) - jnp.log1p(jnp.exp(-jnp.abs(x)))


def _softplus(x):
    return jnp.maximum(x, 0.0) + jnp.log1p(jnp.exp(-jnp.abs(x)))


def _n_levels(chunk):
    return int(np.log2(chunk))


def _prefix_matrix(chunk):
    C, L = chunk, _n_levels(chunk)
    M = np.zeros(((L + 1) * C, C), np.float32)
    M[:C] = np.tril(np.ones((C, C), np.float32))
    for l in range(L):
        m = C >> (l + 1)
        for t in range(C):
            mid = (t // (2 * m)) * 2 * m + m - 1
            if t % (2 * m) >= m:
                M[(l + 1) * C + t, mid + 1:t + 1] = 1.0
            else:
                M[(l + 1) * C + t, t + 1:mid + 1] = 1.0
    return jnp.asarray(M, BF16)


def _block_ones(n_groups, w_in, w_out):
    return jnp.asarray(np.kron(np.eye(n_groups, dtype=np.float32), np.ones((w_in, w_out), np.float32)), BF16)


def _rope_tables(seq):
    half = HEAD_DIM // 2
    inv = ROPE_THETA ** (-np.arange(half, dtype=np.float64) / half)
    ang = np.arange(seq, dtype=np.float64)[:, None] * inv[None, :]
    cos, sin = np.cos(ang), np.sin(ang)
    cos_h = np.concatenate([cos, cos], axis=1)
    sin_h = np.concatenate([-sin, sin], axis=1)
    return (jnp.asarray(np.tile(cos_h, (1, N_HEADS)), F32), jnp.asarray(np.tile(sin_h, (1, N_HEADS)), F32))


QKV_WIDTH = 3 * MIX_WIDTH


_MIX_HG = 0
_MIX_GLA = _MIX_HG + HG_WIDTH
_MIX_RW = _MIX_GLA + -(-GLA_WIDTH // LANES) * LANES
_MIX_AT = _MIX_RW + RW_WIDTH
MIX_PROJ_WIDTH = _MIX_AT + 3 * AT_WIDTH


def _inproj_kernel(x_ref, nw_ref, cos_ref, sin_ref, w_ref,
                   zhg_ref, zgla_ref, zrw_ref, at1_ref, at4_ref, at16_ref, at_scr):
    tm = x_ref.shape[0]
    hb = _rms(x_ref[...], nw_ref[...]).astype(BF16)
    zhg_ref[...] = jnp.dot(hb, w_ref[:, _MIX_HG:_MIX_HG + HG_WIDTH], preferred_element_type=F32)
    zgla_ref[...] = jnp.dot(hb, w_ref[:, _MIX_GLA:_MIX_GLA + GLA_WIDTH], preferred_element_type=F32)
    zrw_ref[...] = jnp.dot(hb, w_ref[:, _MIX_RW:_MIX_RW + RW_WIDTH], preferred_element_type=F32)
    zat = jnp.dot(hb, w_ref[:, _MIX_AT:], preferred_element_type=F32)
    cos, sin = cos_ref[...], sin_ref[...]
    lane = lax.broadcasted_iota(jnp.int32, cos.shape, 1)
    first_half = (lane % HEAD_DIM) < (HEAD_DIM // 2)

    def rope(p):
        partner = jnp.where(first_half, pltpu.roll(p, MIX_WIDTH - HEAD_DIM // 2, 1), pltpu.roll(p, HEAD_DIM // 2, 1))
        return p * cos + partner * sin

    n_lane_blk = QKV_WIDTH // LANES
    for g, (out_ref, (_, dil)) in enumerate(zip((at1_ref, at4_ref, at16_ref), DIL_PAIRS)):
        c0 = g * QKV_WIDTH
        q = rope(zat[:, c0:c0 + MIX_WIDTH]) * (HEAD_DIM ** -0.5 * LOG2_E)
        k = rope(zat[:, c0 + MIX_WIDTH:c0 + 2 * MIX_WIDTH])
        qkv = jnp.concatenate([q, k, zat[:, c0 + 2 * MIX_WIDTH:c0 + QKV_WIDTH]], axis=1)
        if dil == 1:
            out_ref[...] = qkv.astype(BF16)
            continue
        for c in range(n_lane_blk):
            at_scr[c] = qkv[:, c * LANES:(c + 1) * LANES]
        for r in range(dil):
            rows = jnp.concatenate([at_scr[c, pl.ds(r, tm // dil, stride=dil), :] for c in range(n_lane_blk)], axis=1)
            out_ref[:, r * QKV_WIDTH:(r + 1) * QKV_WIDTH] = rows.astype(BF16)


def _inproj(x2, nw, cos_t, sin_t, w_mix, layer, seq):
    T, D = x2.shape
    tm = ROW_TILE
    n_s = seq // tm
    row = lambda w: pl.BlockSpec((tm, w), lambda i: (i, 0))
    pos = pl.BlockSpec((tm, MIX_WIDTH), lambda i: (i % n_s, 0))
    widths = (HG_WIDTH, GLA_WIDTH, RW_WIDTH)
    dils = [d for _, d in DIL_PAIRS]
    return pl.pallas_call(
        _inproj_kernel,
        grid=(T // tm,),
        in_specs=[row(D), _const_spec((1, D)), pos, pos, _layer_spec(w_mix.shape, layer)],
        out_specs=[row(w) for w in widths]
                  + [pl.BlockSpec((tm // d, d * QKV_WIDTH), lambda i: (i, 0)) for d in dils],
        out_shape=[jax.ShapeDtypeStruct((T, w), F32) for w in widths]
                  + [jax.ShapeDtypeStruct((T // d, d * QKV_WIDTH), BF16) for d in dils],
        scratch_shapes=[pltpu.VMEM((QKV_WIDTH // LANES, tm, LANES), F32)],
        compiler_params=_params(1),
        name="inproj",
    )(x2, nw, cos_t, sin_t, w_mix)


def _level_masks(chunk):
    t = lax.broadcasted_iota(jnp.int32, (chunk, chunk), 0)
    s = lax.broadcasted_iota(jnp.int32, (chunk, chunk), 1)
    masks = []
    for l in range(_n_levels(chunk)):
        m = chunk >> (l + 1)
        sh = int(np.log2(2 * m))
        masks.append(((t >> sh) == (s >> sh)) & ((t & m) != 0) & ((s & m) == 0))
    return masks


def _gla_block(q, k, v, g, st_ref, pm_ref, qk_ones, *, chunk, dk, dv):
    C = chunk
    nc = q.shape[0] // C
    n_lvl = _n_levels(C)
    masks = _level_masks(C)
    pm = pm_ref[...]
    rows = lambda t, c: t[c * C:(c + 1) * C]
    stack = lambda f: jnp.concatenate([f(c) for c in range(nc)], axis=0)
    g2 = g * LOG2_E
    D = [_mm_sel_lhs(pm, rows(g2, c)) for c in range(nc)]
    b = stack(lambda c: D[c][:C])
    b_last = stack(lambda c: jnp.broadcast_to(D[c][C - 1:C, :], (C, b.shape[1])))
    dec_last = [jnp.exp2(D[c][C - 1:C, :]) for c in range(nc)]
    q_in = (q * jnp.exp2(b)).astype(BF16)
    k_dec = (k * jnp.exp2(b_last - b)).astype(BF16)
    vb = v.astype(BF16)
    qf, kf = [], []
    for l in range(n_lvl):
        e = jnp.exp2(stack(lambda c: D[c][(l + 1) * C:(l + 2) * C]))
        qf.append((q * e).astype(BF16))
        kf.append((k * e).astype(BF16))
    diag = _mm(q * k, qk_ones)
    pairs = [(c, h) for c in range(nc) for h in range(N_HEADS)]
    bk = lambda t, c, h: t[c * C:(c + 1) * C, h * dk:(h + 1) * dk]
    bv = lambda t, c, h: t[c * C:(c + 1) * C, h * dv:(h + 1) * dv]
    nt = (((1,), (1,)), ((), ()))
    tn = (((0,), (0,)), ((), ()))
    prod = {(p, l): lax.dot_general(bk(qf[l], *p), bk(kf[l], *p), nt, preferred_element_type=F32)
            for p in pairs for l in range(n_lvl)}
    o_intra, kv = {}, {}
    for p in pairs:
        sc = jnp.zeros((C, C), F32)
        for l in range(n_lvl):
            sc = jnp.where(masks[l], prod[p, l], sc)
        o_intra[p] = jnp.dot(sc.astype(BF16), bv(vb, *p), preferred_element_type=F32)
        kv[p] = lax.dot_general(bv(vb, *p), bk(k_dec, *p), tn, preferred_element_type=F32)
    s = [st_ref[h] for h in range(N_HEADS)]
    out_rows = []
    for c in range(nc):
        o_inter = [lax.dot_general(bk(q_in, c, h), s[h].astype(BF16), nt, preferred_element_type=F32)
                   for h in range(N_HEADS)]
        s = [s[h] * dec_last[c][:, h * dk:(h + 1) * dk] + kv[c, h] for h in range(N_HEADS)]
        out_rows.append(jnp.concatenate([o_intra[c, h] + o_inter[h] for h in range(N_HEADS)], axis=1))
    for h in range(N_HEADS):
        st_ref[h] = s[h]
    return jnp.concatenate(out_rows, axis=0) + diag * v


def _head_rmsnorm(o, head_ones, w):
    ms = _mm_sel_rhs(o * o, head_ones) * (1.0 / HEAD_DIM)
    return o * lax.rsqrt(ms + NORM_EPS) * w


def _hgrn2_kernel(z_ref, lbt_ref, nw_ref, pm_ref, ones_ref, o_ref, st_ref, *, layer):
    @pl.when(pl.program_id(1) == 0)
    def _():
        st_ref[...] = jnp.zeros_like(st_ref)

    tab = lbt_ref[...]
    e = jnp.exp(tab - jnp.max(tab, axis=0, keepdims=True))
    p = e / jnp.sum(e, axis=0, keepdims=True)
    lb = jnp.zeros((1, MIX_WIDTH), F32)
    for i in range(1, layer + 1):
        lb = lb + p[i:i + 1, :]
    ones = ones_ref[...]
    z = z_ref[...]
    q_raw, fz = z[:, :MIX_WIDTH], z[:, MIX_WIDTH:2 * MIX_WIDTH]
    i_raw, g_raw = z[:, 2 * MIX_WIDTH:3 * MIX_WIDTH], z[:, 3 * MIX_WIDTH:]
    log_f = _log_sigmoid(fz) + jnp.log1p(lb * jnp.exp(jnp.minimum(-fz, EXP_CLIP)))
    k = (1.0 - lb) * _sigmoid(-fz)
    q = q_raw * _sigmoid(q_raw)
    o = _gla_block(q, k, i_raw, log_f, st_ref, pm_ref, ones, chunk=GLA_CHUNK, dk=HEAD_DIM, dv=HEAD_DIM)
    o_ref[...] = _head_rmsnorm(o, ones, nw_ref[...]) * _sigmoid(g_raw)


def _gla_kernel(z_ref, aw2_ref, ab_ref, nw_ref, pm_ref, qk_ones_ref, ones_ref, o_ref, st_ref):
    @pl.when(pl.program_id(1) == 0)
    def _():
        st_ref[...] = jnp.zeros_like(st_ref)

    z = z_ref[...]
    q = z[:, :GLA_QK] * (GLA_DK ** -0.5)
    k = z[:, GLA_QK:2 * GLA_QK]
    v = z[:, 2 * GLA_QK:2 * GLA_QK + MIX_WIDTH]
    gate = z[:, 2 * GLA_QK + MIX_WIDTH:2 * GLA_QK + 2 * MIX_WIDTH]
    a_low = z[:, 2 * GLA_QK + 2 * MIX_WIDTH:]
    log_alpha = _log_sigmoid(_mm(a_low, aw2_ref[...]) + ab_ref[...]) * (1.0 / GLA_TAU)
    o = _gla_block(q, k, v, log_alpha, st_ref, pm_ref, qk_ones_ref[...], chunk=GLA_CHUNK, dk=GLA_DK, dv=HEAD_DIM)
    o_ref[...] = _head_rmsnorm(o, ones_ref[...], nw_ref[...]) * (gate * _sigmoid(gate))


def _time_grid(T, seq):
    ts = TIME_BLOCK
    n_t = seq // ts
    return ts, n_t, (T // seq, n_t)


def _hgrn2(z_hg, lb_table, norm_w, layer, seq):
    T = z_hg.shape[0]
    ts, n_t, grid = _time_grid(T, seq)
    blk = lambda w: pl.BlockSpec((ts, w), lambda b, i: (b * n_t + i, 0))
    pm = _prefix_matrix(GLA_CHUNK)
    ones = _block_ones(N_HEADS, HEAD_DIM, HEAD_DIM)
    return pl.pallas_call(
        functools.partial(_hgrn2_kernel, layer=layer),
        grid=grid,
        in_specs=[blk(HG_WIDTH), _const_spec(lb_table.shape), _const_spec((1, MIX_WIDTH)),
                  _const_spec(pm.shape), _const_spec(ones.shape)],
        out_specs=blk(MIX_WIDTH),
        out_shape=jax.ShapeDtypeStruct((T, MIX_WIDTH), F32),
        scratch_shapes=[pltpu.VMEM((N_HEADS, HEAD_DIM, HEAD_DIM), F32)],
        compiler_params=_params(2),
        name="hgrn2",
    )(z_hg, lb_table, norm_w, pm, ones)


def _gla(z_gla, a_w2, a_b, norm_w, seq):
    T = z_gla.shape[0]
    ts, n_t, grid = _time_grid(T, seq)
    blk = lambda w: pl.BlockSpec((ts, w), lambda b, i: (b * n_t + i, 0))
    pm = _prefix_matrix(GLA_CHUNK)
    qk_ones = _block_ones(N_HEADS, GLA_DK, HEAD_DIM)
    ones = _block_ones(N_HEADS, HEAD_DIM, HEAD_DIM)
    return pl.pallas_call(
        _gla_kernel,
        grid=grid,
        in_specs=[blk(GLA_WIDTH), _const_spec(a_w2.shape), _const_spec((1, GLA_QK)), _const_spec((1, MIX_WIDTH)),
                  _const_spec(pm.shape), _const_spec(qk_ones.shape), _const_spec(ones.shape)],
        out_specs=blk(MIX_WIDTH),
        out_shape=jax.ShapeDtypeStruct((T, MIX_WIDTH), F32),
        scratch_shapes=[pltpu.VMEM((N_HEADS, HEAD_DIM, GLA_DK), F32)],
        compiler_params=_params(2),
        name="gla",
    )(z_gla, a_w2, a_b, norm_w, pm, qk_ones, ones)


_RW_R, _RW_K, _RW_V = 0, MIX_WIDTH, 2 * MIX_WIDTH
_RW_WL = 3 * MIX_WIDTH
_RW_AL = _RW_WL + RW_DECAY_LORA
_RW_GL = _RW_AL + RW_AAA_LORA


def _rwkv_kernel(*refs, has_vfirst):
    if has_vfirst:
        (z_ref, vf_ref, mu_ref, w0_ref, w2_ref, a0_ref, a2_ref, g2_ref, kk_ref, ka_ref, rk_ref, lnw_ref, lnb_ref,
         v0_ref, v1_ref, v2_ref, tri_ref, ones_ref, o_ref, s_ref, carry_ref) = refs
    else:
        (z_ref, mu_ref, w0_ref, w2_ref, a0_ref, a2_ref, g2_ref, kk_ref, ka_ref, rk_ref, lnw_ref, lnb_ref,
         tri_ref, ones_ref, o_ref, vout_ref, s_ref, carry_ref) = refs
    C, HD = RW_CHUNK, HEAD_DIM
    ts = z_ref.shape[0]
    nc = ts // C
    pairs = [(c, h) for c in range(nc) for h in range(N_HEADS)]

    @pl.when(pl.program_id(1) == 0)
    def _():
        s_ref[...] = jnp.zeros_like(s_ref)
        carry_ref[...] = jnp.zeros_like(carry_ref)

    z = z_ref[...]
    zp = pltpu.roll(z, 1, 0)
    zp = jnp.where(lax.broadcasted_iota(jnp.int32, z.shape, 0) == 0, carry_ref[...], zp)
    carry_ref[...] = z[ts - 1:ts, :]
    zs = z + mu_ref[...] * (zp - z)
    r, k, v = zs[:, _RW_R:_RW_K], zs[:, _RW_K:_RW_V], zs[:, _RW_V:_RW_WL]
    wl, al, gl = zs[:, _RW_WL:_RW_AL], zs[:, _RW_AL:_RW_GL], zs[:, _RW_GL:]
    ones = ones_ref[...]
    gsum = lambda t: _mm_sel_rhs(t, ones)
    w = -_softplus(-(w0_ref[...] + _mm(jnp.tanh(wl), w2_ref[...]))) - 0.5
    lw = -jnp.exp(w)
    a = _sigmoid(a0_ref[...] + _mm(al, a2_ref[...]))
    g = _mm(_sigmoid(gl), g2_ref[...])
    if has_vfirst:
        v = v + (vf_ref[...] - v) * _sigmoid(v0_ref[...] + _mm(_mm(v, v1_ref[...]), v2_ref[...]))
    else:
        vout_ref[...] = v
    kk = k * kk_ref[...]
    kk = kk / jnp.maximum(jnp.sqrt(gsum(kk * kk)), 1e-12)
    k_mod = k * (1.0 + (a - 1.0) * ka_ref[...])
    kka = kk * a

    tri = tri_ref[...]
    cums = [_mm_sel_lhs(tri, lw[c * C:(c + 1) * C]) for c in range(nc)]
    cum = jnp.concatenate(cums, axis=0)
    cum_last = jnp.concatenate([jnp.broadcast_to(cc[C - 1:C, :], (C, MIX_WIDTH)) for cc in cums], axis=0)
    w_end = [jnp.exp(cc[C - 1:C, :]) for cc in cums]
    r_t = r * jnp.exp(cum)
    a_t = -kk * jnp.exp(cum - lw)
    inv = jnp.exp(-cum)
    end = jnp.exp(cum_last - cum)
    r_tb, a_tb = r_t.astype(BF16), a_t.astype(BF16)
    k_hb, b_hb = (k_mod * inv).astype(BF16), (kka * inv).astype(BF16)
    k_eb, b_eb = (k_mod * end).astype(BF16), (kka * end).astype(BF16)
    vb = v.astype(BF16)
    blk = lambda t, c, h: t[c * C:(c + 1) * C, h * HD:(h + 1) * HD]

    ti = lax.broadcasted_iota(jnp.int32, (C, C), 0)
    si = lax.broadcasted_iota(jnp.int32, (C, C), 1)
    incl, strict = si <= ti, si < ti
    dot = lambda x, y: jnp.dot(x, y, preferred_element_type=F32)
    aa = {p: _mm_nt(jnp.concatenate([blk(r_tb, *p), blk(a_tb, *p)], axis=0),
                    jnp.concatenate([blk(k_hb, *p), blk(b_hb, *p)], axis=0)) for p in pairs}
    incl2 = (lax.broadcasted_iota(jnp.int32, (C, 2 * C), 1) & (C - 1)) <= lax.broadcasted_iota(jnp.int32, (C, 2 * C), 0)
    a_r = {p: jnp.where(incl2, aa[p][:C], 0.0).astype(BF16) for p in pairs}
    a_ak = {p: jnp.where(strict, aa[p][C:, :C], 0.0).astype(BF16) for p in pairs}
    npow = {p: jnp.where(strict, aa[p][C:, C:], 0.0) for p in pairs}
    pv = {p: dot(a_ak[p], blk(vb, *p)) for p in pairs}
    x = {p: jnp.concatenate([blk(a_t, *p), pv[p]], axis=1) for p in pairs}
    n_stage = _n_levels(C)
    for j in range(n_stage):
        last = j == n_stage - 1
        nb = {p: npow[p].astype(BF16) for p in pairs}
        nx = {p: dot(nb[p], x[p].astype(BF16)) for p in pairs}
        if not last:
            npow = {p: dot(nb[p], nb[p]) for p in pairs}
        x = {p: x[p] + nx[p] for p in pairs}
    rmat = {p: jnp.concatenate([jnp.concatenate([jnp.zeros((C, HD), BF16), blk(vb, *p)], axis=1), x[p].astype(BF16)],
                               axis=0) for p in pairs}
    qz = {p: dot(a_r[p], rmat[p]) for p in pairs}
    gh = {p: lax.dot_general(rmat[p], jnp.concatenate([blk(k_eb, *p), blk(b_eb, *p)], axis=0),
                             (((0,), (0,)), ((), ())), preferred_element_type=F32) for p in pairs}
    q_f = {p: (blk(r_t, *p) + qz[p][:, :HD]).astype(BF16) for p in pairs}

    s = [s_ref[h] for h in range(N_HEADS)]
    y_rows = []
    for c in range(nc):
        sb = [s[h].astype(BF16) for h in range(N_HEADS)]
        ys = [lax.dot_general(q_f[c, h], sb[h], (((1,), (1,)), ((), ())), preferred_element_type=F32)
              + qz[c, h][:, HD:] for h in range(N_HEADS)]
        sg = [dot(sb[h], gh[c, h][:HD].astype(BF16)) for h in range(N_HEADS)]
        s = [s[h] * w_end[c][:, h * HD:(h + 1) * HD] + sg[h] + gh[c, h][HD:] for h in range(N_HEADS)]
        y_rows.append(jnp.concatenate(ys, axis=1))
    for h in range(N_HEADS):
        s_ref[h] = s[h]
    y = jnp.concatenate(y_rows, axis=0)

    mean = gsum(y) * (1.0 / HD)
    yc = y - mean
    var = gsum(yc * yc) * (1.0 / HD)
    yn = yc * lax.rsqrt(var + RW_GN_EPS) * lnw_ref[...] + lnb_ref[...]
    bonus = gsum(r * k_mod * rk_ref[...]) * v
    o_ref[...] = (yn + bonus) * g


def _rwkv(z_rw, v_first, p, seq):
    T = z_rw.shape[0]
    ts, n_t, grid = _time_grid(T, seq)
    blk = lambda w: pl.BlockSpec((ts, w), lambda b, i: (b * n_t + i, 0))
    tri = jnp.asarray(np.tril(np.ones((RW_CHUNK, RW_CHUNK), np.float32)), BF16)
    ones = _block_ones(N_HEADS, HEAD_DIM, HEAD_DIM)
    has_vfirst = v_first is not None
    names = ["mu", "w0", "w2", "a0", "a2", "g2", "k_k", "k_a", "r_k", "ln_w", "ln_b"]
    args = [z_rw] + ([v_first] if has_vfirst else []) + [p[n] for n in names]
    specs = [blk(RW_WIDTH)] + ([blk(MIX_WIDTH)] if has_vfirst else []) + [_const_spec(p[n].shape) for n in names]
    if has_vfirst:
        args += [p["v0"], p["v1"], p["v2"]]
        specs += [_const_spec(p[n].shape) for n in ("v0", "v1", "v2")]
    args += [tri, ones]
    specs += [_const_spec(tri.shape), _const_spec(ones.shape)]
    out_sds = jax.ShapeDtypeStruct((T, MIX_WIDTH), F32)
    res = pl.pallas_call(
        functools.partial(_rwkv_kernel, has_vfirst=has_vfirst),
        grid=grid,
        in_specs=specs,
        out_specs=blk(MIX_WIDTH) if has_vfirst else [blk(MIX_WIDTH), blk(MIX_WIDTH)],
        out_shape=out_sds if has_vfirst else [out_sds, out_sds],
        scratch_shapes=[pltpu.VMEM((N_HEADS, HEAD_DIM, HEAD_DIM), F32), pltpu.VMEM((1, RW_WIDTH), F32)],
        compiler_params=_params(2),
        name="rwkv7",
    )(*args)
    return (res, None) if has_vfirst else (res[0], res[1])


ATT_TOKENS = ATT_BLOCK * max(d for _, d in DIL_PAIRS)
ATT_UNITS_PER_BATCH = 4


def _attn_kernel(c1_ref, p1_ref, c4_ref, p4_ref, c16_ref, p16_ref, o_ref, acc_scr, m_scr, l_scr):
    qb = ATT_BLOCK
    n_tile = MIX_WIDTH // LANES
    row = lax.broadcasted_iota(jnp.int32, (qb, 2 * qb), 0)
    col = lax.broadcasted_iota(jnp.int32, (qb, 2 * qb), 1)
    band = (col >= row) & (col <= row + qb)
    band_first = band & (col >= jnp.where(pl.program_id(1) > 0, 0, qb))
    low = lax.broadcasted_iota(jnp.int32, (qb, LANES), 1) < HEAD_DIM
    nt = (((1,), (1,)), ((), ()))
    curs, prevs = (c1_ref, c4_ref, c16_ref), (p1_ref, p4_ref, p16_ref)
    units = [(g, dil, r, qi) for g, (_, dil) in enumerate(DIL_PAIRS) for r in range(dil)
             for qi in range(ATT_TOKENS // dil // qb)]
    pairs = [(t, e) for t in range(n_tile) for e in range(LANES // HEAD_DIM)]
    tile = lambda x, t: x[:, t * LANES:(t + 1) * LANES]
    for b0 in range(0, len(units), ATT_UNITS_PER_BATCH):
        batch = units[b0:b0 + ATT_UNITS_PER_BATCH]
        vs, masks, scores = [], [], []
        for g, dil, r, qi in batch:
            cur, prev = curs[g], prevs[g]
            qc, kc, vc = (r * QKV_WIDTH + j * MIX_WIDTH for j in range(3))
            q = cur[0, qi * qb:(qi + 1) * qb, qc:qc + MIX_WIDTH]
            if qi == 0:
                k = jnp.concatenate([prev[0, :, kc:kc + MIX_WIDTH], cur[0, :qb, kc:kc + MIX_WIDTH]], axis=0)
                v = jnp.concatenate([prev[0, :, vc:vc + MIX_WIDTH], cur[0, :qb, vc:vc + MIX_WIDTH]], axis=0)
            else:
                k = cur[0, (qi - 1) * qb:(qi + 1) * qb, kc:kc + MIX_WIDTH]
                v = cur[0, (qi - 1) * qb:(qi + 1) * qb, vc:vc + MIX_WIDTH]
            vs.append(v)
            masks.append(band_first if qi == 0 else band)
            zero = jnp.zeros((qb, LANES), q.dtype)
            scores.append({(t, e): lax.dot_general(jnp.where(low if e == 0 else ~low, tile(q, t), zero), tile(k, t), nt,
                                                   preferred_element_type=F32) for t, e in pairs})
        probs, stats = [], []
        for u in range(len(batch)):
            pu, su = {}, {}
            for p in pairs:
                s = jnp.where(masks[u], scores[u][p], MASK_VALUE)
                m = jnp.max(s, axis=-1, keepdims=True)
                ex = jnp.exp2(s - m)
                su[p] = (m, jnp.sum(ex, axis=-1, keepdims=True))
                pu[p] = ex.astype(BF16)
            probs.append(pu)
            stats.append(su)
        pvs = [{(t, e): jnp.dot(probs[u][t, e], tile(vs[u], t), preferred_element_type=F32) for t, e in pairs}
               for u in range(len(batch))]
        for u, (g, dil, r, qi) in enumerate(batch):
            tok = pl.ds(qi * qb * dil + r, qb, stride=dil) if dil > 1 else pl.ds(qi * qb, qb)
            for t in range(n_tile):
                acc = jnp.where(low, pvs[u][t, 0], pvs[u][t, 1])
                m_new = jnp.where(low, stats[u][t, 0][0], stats[u][t, 1][0])
                l_new = jnp.where(low, stats[u][t, 0][1], stats[u][t, 1][1])
                if g > 0:
                    m_old = m_scr[t, tok, :]
                    m_tot = jnp.maximum(m_old, m_new)
                    e_old, e_new = jnp.exp2(m_old - m_tot), jnp.exp2(m_new - m_tot)
                    acc = acc_scr[t, tok, :] * e_old + acc * e_new
                    l_new = l_scr[t, tok, :] * e_old + l_new * e_new
                    m_new = m_tot
                acc_scr[t, tok, :] = acc
                m_scr[t, tok, :] = m_new
                l_scr[t, tok, :] = l_new
    for t in range(n_tile):
        o_ref[:, t * LANES:(t + 1) * LANES] = acc_scr[t] / l_scr[t]


def _attention(at, batch, seq):
    T = batch * seq
    nblk = seq // ATT_TOKENS
    args, specs = [], []
    for a, (_, d) in zip(at, DIL_PAIRS):
        rows = ATT_TOKENS // d
        n_prev = rows // ATT_BLOCK
        a3 = a.reshape(batch, seq // d, d * QKV_WIDTH)
        args += [a3, a3]
        specs += [pl.BlockSpec((1, rows, d * QKV_WIDTH), lambda b, i: (b, i, 0)),
                  pl.BlockSpec((1, ATT_BLOCK, d * QKV_WIDTH), lambda b, i, n=n_prev: (b, jnp.maximum(i * n - 1, 0), 0))]
    return pl.pallas_call(
        _attn_kernel,
        grid=(batch, nblk),
        in_specs=specs,
        out_specs=pl.BlockSpec((ATT_TOKENS, MIX_WIDTH), lambda b, i: (b * nblk + i, 0)),
        out_shape=jax.ShapeDtypeStruct((T, MIX_WIDTH), F32),
        scratch_shapes=[pltpu.VMEM((MIX_WIDTH // LANES, ATT_TOKENS, LANES), F32)] * 3,
        compiler_params=_params(2),
        name="dilated_attn",
    )(*args)


def _merge_kernel(x_ref, nw_ref, wg_ref, oa_ref, ob_ref, oc_ref, od_ref, wb_ref, wo_ref, out_ref):
    x = x_ref[...]
    hb = _rms(x, nw_ref[...]).astype(BF16)
    acc = jnp.zeros(x.shape, F32)
    for kbr, o in enumerate((oa_ref[...], ob_ref[...], oc_ref[...], od_ref[...])):
        gate = _sigmoid(jnp.dot(hb, wg_ref[:, kbr * D_MODEL:(kbr + 1) * D_MODEL], preferred_element_type=F32))
        acc = acc + gate * _mm(o, wb_ref[kbr])
    out_ref[...] = x + _mm(acc, wo_ref[...])


def _merge(x2, nw, w_gate, o_a, o_b, o_c, o_d, w_branch, w_out, layer):
    T, D = x2.shape
    tm = ROW_TILE
    row = lambda w: pl.BlockSpec((tm, w), lambda i: (i, 0))
    return pl.pallas_call(
        _merge_kernel,
        grid=(T // tm,),
        in_specs=[row(D), _const_spec((1, D)), _layer_spec(w_gate.shape, layer)] + [row(MIX_WIDTH)] * N_BRANCH
                 + [_layer_spec(w_branch.shape, layer), _layer_spec(w_out.shape, layer)],
        out_specs=row(D),
        out_shape=jax.ShapeDtypeStruct((T, D), F32),
        compiler_params=_params(1),
        name="merge",
    )(x2, nw, w_gate, o_a, o_b, o_c, o_d, w_branch, w_out)


def _ffn_kernel(x_ref, nw_ref, wup_ref, cw_ref, cb_ref, wdn_ref, fw_ref, out_ref, tail_ref, *, n_s, final_norm):
    tm = x_ref.shape[0]

    @pl.when(pl.program_id(0) % n_s == 0)
    def _():
        tail_ref[...] = jnp.zeros_like(tail_ref)

    x = x_ref[...]
    hb = _rms(x, nw_ref[...]).astype(BF16)
    up = jnp.dot(hb, wup_ref[:, :D_FF], preferred_element_type=F32)
    gate = jnp.dot(hb, wup_ref[:, D_FF:], preferred_element_type=F32)
    row = lax.broadcasted_iota(jnp.int32, up.shape, 0)
    tail = tail_ref[...]
    up1 = jnp.where(row == 0, tail[1:2, :], pltpu.roll(up, 1, 0))
    up2 = jnp.where(row == 0, tail[0:1, :], jnp.where(row == 1, tail[1:2, :], pltpu.roll(up, 2, 0)))
    tail_ref[...] = up[tm - 2:tm, :]
    cw = cw_ref[...]
    conv = cb_ref[...] + cw[0:1, :] * up + cw[1:2, :] * up1 + cw[2:3, :] * up2
    act = conv * _sigmoid(conv) * gate
    y = x + _mm(act, wdn_ref[...])
    out_ref[...] = _rms(y, fw_ref[...]) if final_norm else y


def _ffn(x2, nw, w_up, conv_w, conv_b, w_down, final_w, layer, seq, final_norm):
    T, D = x2.shape
    tm = ROW_TILE
    row = pl.BlockSpec((tm, D), lambda i: (i, 0))
    return pl.pallas_call(
        functools.partial(_ffn_kernel, n_s=seq // tm, final_norm=final_norm),
        grid=(T // tm,),
        in_specs=[row, _const_spec((1, D)), _layer_spec(w_up.shape, layer), _const_spec(conv_w.shape),
                  _const_spec((1, D_FF)), _layer_spec(w_down.shape, layer), _const_spec((1, D))],
        out_specs=row,
        out_shape=jax.ShapeDtypeStruct((T, D), F32),
        scratch_shapes=[pltpu.VMEM((CONV_W - 1, D_FF), F32)],
        compiler_params=_params(1),
        name="convglu",
    )(x2, nw, w_up, conv_w, conv_b, w_down, final_w)


def kernel(x, norm_mix_w, norm_ffn_w, norm_final_w, w_in, hg_lb_table, hg_norm_w, gla_a_w2, gla_a_b, gla_norm_w, rw_mu, rw_w0, rw_w2, rw_a0, rw_a2, rw_g2, rw_k_k, rw_k_a, rw_r_k, rw_ln_w, rw_ln_b, rw_v0, rw_v1, rw_v2, w_branch, w_out, ffn_w_up, ffn_conv_w, ffn_conv_b, ffn_w_down):
    B, S, D = x.shape
    T = B * S
    x2 = x.reshape(T, D)
    cos_t, sin_t = _rope_tables(S)
    row = lambda t: t.reshape(1, -1)
    W = MIX_WIDTH
    src = np.concatenate([np.arange(0, W), np.arange(W + RW_DECAY_LORA, 3 * W + RW_DECAY_LORA),
                          np.arange(W, W + RW_DECAY_LORA), np.arange(3 * W + RW_DECAY_LORA, RW_WIDTH)])
    rw_runs = [(0, W), (W + RW_DECAY_LORA, 3 * W + RW_DECAY_LORA), (W, W + RW_DECAY_LORA),
               (3 * W + RW_DECAY_LORA, RW_WIDTH)]
    at_runs = [(j * AT_WIDTH + g * W, j * AT_WIDTH + (g + 1) * W) for g in range(N_DIL) for j in range(3)]
    o_hg, o_gla, o_rw, o_at = 0, HG_WIDTH, HG_WIDTH + GLA_WIDTH, HG_WIDTH + GLA_WIDTH + RW_WIDTH
    o_gate = o_at + 3 * AT_WIDTH
    runs = ([(o_hg, o_rw)] + [None] + [(o_rw + a, o_rw + b) for a, b in rw_runs]
            + [(o_at + a, o_at + b) for a, b in at_runs])
    gap = jnp.zeros((DEPTH, D, _MIX_RW - _MIX_GLA - GLA_WIDTH), w_in.dtype)
    w_mix = jnp.concatenate([gap if r is None else w_in[:, :, r[0]:r[1]] for r in runs], axis=2).astype(BF16)
    w_gate = w_in[:, :, o_gate:].astype(BF16)
    w_branch_b, w_out_b = w_branch.astype(BF16), w_out.astype(BF16)
    w_up_b, w_down_b = ffn_w_up.astype(BF16), ffn_w_down.astype(BF16)
    v_first = None
    for layer in range(DEPTH):
        z_hg, z_gla, z_rw, *at = _inproj(x2, row(norm_mix_w[layer]), cos_t, sin_t, w_mix, layer, S)

        o_a = _hgrn2(z_hg, hg_lb_table, row(hg_norm_w[layer]), layer, S)
        o_b = _gla(z_gla, gla_a_w2[layer].astype(BF16), row(gla_a_b[layer]), row(gla_norm_w[layer]), S)
        p = {"mu": row(rw_mu[layer][src]), "w0": row(rw_w0[layer]), "w2": rw_w2[layer].astype(BF16),
             "a0": row(rw_a0[layer]), "a2": rw_a2[layer].astype(BF16), "g2": rw_g2[layer].astype(BF16),
             "k_k": row(rw_k_k[layer]), "k_a": row(rw_k_a[layer]), "r_k": row(rw_r_k[layer]),
             "ln_w": row(rw_ln_w[layer]), "ln_b": row(rw_ln_b[layer])}
        if layer > 0:
            p.update(v0=row(rw_v0[layer - 1]), v1=rw_v1[layer - 1].astype(BF16), v2=rw_v2[layer - 1].astype(BF16))
        o_c, v_new = _rwkv(z_rw, v_first, p, S)
        if layer == 0:
            v_first = v_new
        o_d = _attention(at, B, S)

        x2 = _merge(x2, row(norm_mix_w[layer]), w_gate, o_a, o_b, o_c, o_d, w_branch_b, w_out_b, layer)
        x2 = _ffn(x2, row(norm_ffn_w[layer]), w_up_b, ffn_conv_w[layer],
                  row(ffn_conv_b[layer]), w_down_b, row(norm_final_w), layer, S,
                  final_norm=(layer == DEPTH - 1))
    return x2.reshape(B, S, D)
```

```python
import functools

import numpy as np
import jax
import jax.numpy as jnp
from jax import lax
from jax.experimental import pallas as pl
from jax.experimental.pallas import tpu as pltpu

F32 = jnp.float32
BF16 = jnp.bfloat16

D_MODEL = 1024
DEPTH = 2
HEAD_DIM = 64
MIX_WIDTH = D_MODEL // 4
N_BRANCH = 4
NORM_EPS = 1e-6
MASK_VALUE = -1e9
EXP_CLIP = 60.0
N_HEADS = MIX_WIDTH // HEAD_DIM
GLA_DK = 32
GLA_RANK = 16
GLA_TAU = 16.0
RW_DECAY_LORA = 32
RW_AAA_LORA = 32
RW_MV_LORA = 32
RW_GATE_LORA = 64
RW_GN_EPS = 64e-5
DIL_PAIRS = ((128, 1), (512, 4), (2048, 16))
N_DIL = 3
AT_WIDTH = N_DIL * N_HEADS * HEAD_DIM
ROPE_THETA = 10000.0
D_FF = 11 * D_MODEL // 4
CONV_W = 3

HG_WIDTH = 4 * MIX_WIDTH
GLA_QK = N_HEADS * GLA_DK
GLA_WIDTH = 2 * GLA_QK + 2 * MIX_WIDTH + GLA_RANK
RW_WIDTH = 3 * MIX_WIDTH + RW_DECAY_LORA + RW_AAA_LORA + RW_GATE_LORA
GATE_WIDTH = N_BRANCH * D_MODEL

LANES = 128
LOG2_E = 1.4426950408889634
VMEM_LIMIT_BYTES = 56 * 1024 * 1024
ROW_TILE = 512
GLA_TIME_BLOCK = 1024
RW_TIME_BLOCK = 512
GLA_CHUNK = 128
RW_CHUNK = 64
ATT_BLOCK = 128


def _const_spec(shape):
    nd = len(shape)
    return pl.BlockSpec(shape, lambda *_: (0,) * nd, pipeline_mode=pl.Buffered(1))


def _layer_spec(shape, layer):
    nd = len(shape)
    return pl.BlockSpec((None,) + tuple(shape[1:]), lambda *_: (layer,) + (0,) * (nd - 1), pipeline_mode=pl.Buffered(1))


def _params(n_grid):
    return pltpu.CompilerParams(dimension_semantics=("arbitrary",) * n_grid, vmem_limit_bytes=VMEM_LIMIT_BYTES)


def _mm(a, b):
    return jnp.dot(a.astype(BF16), b.astype(BF16), preferred_element_type=F32)


def _mm_nt(a, b):
    return lax.dot_general(a.astype(BF16), b.astype(BF16), (((1,), (1,)), ((), ())), preferred_element_type=F32)


def _mm_tn(a, b):
    return lax.dot_general(a.astype(BF16), b.astype(BF16), (((0,), (0,)), ((), ())), preferred_element_type=F32)


def _split2(a):
    hi = a.astype(BF16)
    lo = (a - hi.astype(F32)).astype(BF16)
    return hi, lo


def _mm_sel_lhs(sel, a):
    hi, lo = _split2(a)
    return jnp.dot(sel, hi, preferred_element_type=F32) + jnp.dot(sel, lo, preferred_element_type=F32)


def _mm_sel_rhs(a, sel):
    hi, lo = _split2(a)
    return jnp.dot(hi, sel, preferred_element_type=F32) + jnp.dot(lo, sel, preferred_element_type=F32)


def _rms(x, w):
    return x * lax.rsqrt(jnp.mean(x * x, axis=-1, keepdims=True) + NORM_EPS) * w


def _sigmoid(x):
    return jax.nn.sigmoid(x)


def _log_sigmoid(x):
    return jnp.minimum(x, 0.0) - jnp.log1p(jnp.exp(-jnp.abs(x)))


def _softplus(x):
    return jnp.maximum(x, 0.0) + jnp.log1p(jnp.exp(-jnp.abs(x)))


def _n_levels(chunk):
    return int(np.log2(chunk))


def _prefix_matrix(chunk):
    C, L = chunk, _n_levels(chunk)
    M = np.zeros(((L + 1) * C, C), np.float32)
    M[:C] = np.tril(np.ones((C, C), np.float32))
    for l in range(L):
        m = C >> (l + 1)
        for t in range(C):
            mid = (t // (2 * m)) * 2 * m + m - 1
            if t % (2 * m) >= m:
                M[(l + 1) * C + t, mid + 1:t + 1] = 1.0
            else:
                M[(l + 1) * C + t, t + 1:mid + 1] = 1.0
    return jnp.asarray(M, BF16)


def _block_ones(n_groups, w_in, w_out):
    return jnp.asarray(np.kron(np.eye(n_groups, dtype=np.float32), np.ones((w_in, w_out), np.float32)), BF16)


def _rope_tables(seq):
    half = HEAD_DIM // 2
    inv = ROPE_THETA ** (-np.arange(half, dtype=np.float64) / half)
    ang = np.arange(seq, dtype=np.float64)[:, None] * inv[None, :]
    cos, sin = np.cos(ang), np.sin(ang)
    cos_h = np.concatenate([cos, cos], axis=1)
    sin_h = np.concatenate([-sin, sin], axis=1)
    return (jnp.asarray(np.tile(cos_h, (1, N_HEADS)), F32), jnp.asarray(np.tile(sin_h, (1, N_HEADS)), F32))


QKV_WIDTH = 3 * MIX_WIDTH


_MIX_HG = 0
_MIX_GLA = _MIX_HG + HG_WIDTH
_MIX_RW = _MIX_GLA + -(-GLA_WIDTH // LANES) * LANES
_MIX_AT = _MIX_RW + RW_WIDTH
MIX_PROJ_WIDTH = _MIX_AT + 3 * AT_WIDTH


def _inproj_kernel(x_ref, nw_ref, cos_ref, sin_ref, w_ref,
                   zhg_ref, zgla_ref, zrw_ref, at1_ref, at4_ref, at16_ref, at_scr):
    tm = x_ref.shape[0]
    hb = _rms(x_ref[...], nw_ref[...]).astype(BF16)
    zhg_ref[...] = jnp.dot(hb, w_ref[:, _MIX_HG:_MIX_HG + HG_WIDTH], preferred_element_type=F32)
    zgla_ref[...] = jnp.dot(hb, w_ref[:, _MIX_GLA:_MIX_GLA + GLA_WIDTH], preferred_element_type=F32)
    zrw_ref[...] = jnp.dot(hb, w_ref[:, _MIX_RW:_MIX_RW + RW_WIDTH], preferred_element_type=F32)
    zat = jnp.dot(hb, w_ref[:, _MIX_AT:], preferred_element_type=F32)
    cos, sin = cos_ref[...], sin_ref[...]
    lane = lax.broadcasted_iota(jnp.int32, cos.shape, 1)
    first_half = (lane % HEAD_DIM) < (HEAD_DIM // 2)

    def rope(p):
        partner = jnp.where(first_half, pltpu.roll(p, MIX_WIDTH - HEAD_DIM // 2, 1), pltpu.roll(p, HEAD_DIM // 2, 1))
        return p * cos + partner * sin

    n_lane_blk = QKV_WIDTH // LANES
    for g, (out_ref, (_, dil)) in enumerate(zip((at1_ref, at4_ref, at16_ref), DIL_PAIRS)):
        c0 = g * QKV_WIDTH
        q = rope(zat[:, c0:c0 + MIX_WIDTH]) * (HEAD_DIM ** -0.5 * LOG2_E)
        k = rope(zat[:, c0 + MIX_WIDTH:c0 + 2 * MIX_WIDTH])
        qkv = jnp.concatenate([q, k, zat[:, c0 + 2 * MIX_WIDTH:c0 + QKV_WIDTH]], axis=1)
        if dil == 1:
            out_ref[...] = qkv.astype(BF16)
            continue
        for c in range(n_lane_blk):
            at_scr[c] = qkv[:, c * LANES:(c + 1) * LANES]
        for r in range(dil):
            rows = jnp.concatenate([at_scr[c, pl.ds(r, tm // dil, stride=dil), :] for c in range(n_lane_blk)], axis=1)
            out_ref[:, r * QKV_WIDTH:(r + 1) * QKV_WIDTH] = rows.astype(BF16)


def _inproj(x2, nw, cos_t, sin_t, w_mix, layer, seq):
    T, D = x2.shape
    tm = ROW_TILE
    n_s = seq // tm
    row = lambda w: pl.BlockSpec((tm, w), lambda i: (i, 0))
    pos = pl.BlockSpec((tm, MIX_WIDTH), lambda i: (i % n_s, 0))
    widths = (HG_WIDTH, GLA_WIDTH, RW_WIDTH)
    dils = [d for _, d in DIL_PAIRS]
    return pl.pallas_call(
        _inproj_kernel,
        grid=(T // tm,),
        in_specs=[row(D), _const_spec((1, D)), pos, pos, _layer_spec(w_mix.shape, layer)],
        out_specs=[row(w) for w in widths]
                  + [pl.BlockSpec((tm // d, d * QKV_WIDTH), lambda i: (i, 0)) for d in dils],
        out_shape=[jax.ShapeDtypeStruct((T, w), F32) for w in widths]
                  + [jax.ShapeDtypeStruct((T // d, d * QKV_WIDTH), BF16) for d in dils],
        scratch_shapes=[pltpu.VMEM((QKV_WIDTH // LANES, tm, LANES), F32)],
        compiler_params=_params(1),
        name="inproj",
    )(x2, nw, cos_t, sin_t, w_mix)


def _level_masks(chunk):
    t = lax.broadcasted_iota(jnp.int32, (chunk, chunk), 0)
    s = lax.broadcasted_iota(jnp.int32, (chunk, chunk), 1)
    masks = []
    for l in range(_n_levels(chunk)):
        m = chunk >> (l + 1)
        sh = int(np.log2(2 * m))
        masks.append(((t >> sh) == (s >> sh)) & ((t & m) != 0) & ((s & m) == 0))
    return masks


def _gla_block(q, k, v, g, st_ref, pm_ref, qk_ones, *, chunk, dk, dv):
    C = chunk
    nc = q.shape[0] // C
    n_lvl = _n_levels(C)
    masks = _level_masks(C)
    pm = pm_ref[...]
    rows = lambda t, c: t[c * C:(c + 1) * C]
    stack = lambda f: jnp.concatenate([f(c) for c in range(nc)], axis=0)
    g2 = g * LOG2_E
    D = [_mm_sel_lhs(pm, rows(g2, c)) for c in range(nc)]
    b = stack(lambda c: D[c][:C])
    b_last = stack(lambda c: jnp.broadcast_to(D[c][C - 1:C, :], (C, b.shape[1])))
    dec_last = [jnp.exp2(D[c][C - 1:C, :]) for c in range(nc)]
    q_in = (q * jnp.exp2(b)).astype(BF16)
    k_dec = (k * jnp.exp2(b_last - b)).astype(BF16)
    vb = v.astype(BF16)
    qf, kf = [], []
    for l in range(n_lvl):
        e = jnp.exp2(stack(lambda c: D[c][(l + 1) * C:(l + 2) * C]))
        qf.append((q * e).astype(BF16))
        kf.append((k * e).astype(BF16))
    diag = _mm(q * k, qk_ones)
    pairs = [(c, h) for c in range(nc) for h in range(N_HEADS)]
    bk = lambda t, c, h: t[c * C:(c + 1) * C, h * dk:(h + 1) * dk]
    bv = lambda t, c, h: t[c * C:(c + 1) * C, h * dv:(h + 1) * dv]
    nt = (((1,), (1,)), ((), ()))
    tn = (((0,), (0,)), ((), ()))
    prod = {(p, l): lax.dot_general(bk(qf[l], *p), bk(kf[l], *p), nt, preferred_element_type=F32)
            for p in pairs for l in range(n_lvl)}
    o_intra, kv = {}, {}
    for p in pairs:
        sc = jnp.zeros((C, C), F32)
        for l in range(n_lvl):
            sc = jnp.where(masks[l], prod[p, l], sc)
        o_intra[p] = jnp.dot(sc.astype(BF16), bv(vb, *p), preferred_element_type=F32)
        kv[p] = lax.dot_general(bv(vb, *p), bk(k_dec, *p), tn, preferred_element_type=F32)
    s = [st_ref[h] for h in range(N_HEADS)]
    out_rows = []
    for c in range(nc):
        o_inter = [lax.dot_general(bk(q_in, c, h), s[h].astype(BF16), nt, preferred_element_type=F32)
                   for h in range(N_HEADS)]
        s = [s[h] * dec_last[c][:, h * dk:(h + 1) * dk] + kv[c, h] for h in range(N_HEADS)]
        out_rows.append(jnp.concatenate([o_intra[c, h] + o_inter[h] for h in range(N_HEADS)], axis=1))
    for h in range(N_HEADS):
        st_ref[h] = s[h]
    return jnp.concatenate(out_rows, axis=0) + diag * v


def _head_rmsnorm(o, head_ones, w):
    ms = _mm_sel_rhs(o * o, head_ones) * (1.0 / HEAD_DIM)
    return o * lax.rsqrt(ms + NORM_EPS) * w


def _hgrn2_kernel(z_ref, lbt_ref, nw_ref, pm_ref, ones_ref, o_ref, st_ref, *, layer):
    @pl.when(pl.program_id(1) == 0)
    def _():
        st_ref[...] = jnp.zeros_like(st_ref)

    tab = lbt_ref[...]
    e = jnp.exp(tab - jnp.max(tab, axis=0, keepdims=True))
    p = e / jnp.sum(e, axis=0, keepdims=True)
    lb = jnp.zeros((1, MIX_WIDTH), F32)
    for i in range(1, layer + 1):
        lb = lb + p[i:i + 1, :]
    ones = ones_ref[...]
    z = z_ref[...]
    q_raw, fz = z[:, :MIX_WIDTH], z[:, MIX_WIDTH:2 * MIX_WIDTH]
    i_raw, g_raw = z[:, 2 * MIX_WIDTH:3 * MIX_WIDTH], z[:, 3 * MIX_WIDTH:]
    log_f = _log_sigmoid(fz) + jnp.log1p(lb * jnp.exp(jnp.minimum(-fz, EXP_CLIP)))
    k = (1.0 - lb) * _sigmoid(-fz)
    q = q_raw * _sigmoid(q_raw)
    o = _gla_block(q, k, i_raw, log_f, st_ref, pm_ref, ones, chunk=GLA_CHUNK, dk=HEAD_DIM, dv=HEAD_DIM)
    o_ref[...] = _head_rmsnorm(o, ones, nw_ref[...]) * _sigmoid(g_raw)


def _gla_kernel(z_ref, aw2_ref, ab_ref, nw_ref, pm_ref, qk_ones_ref, ones_ref, o_ref, st_ref):
    @pl.when(pl.program_id(1) == 0)
    def _():
        st_ref[...] = jnp.zeros_like(st_ref)

    z = z_ref[...]
    q = z[:, :GLA_QK] * (GLA_DK ** -0.5)
    k = z[:, GLA_QK:2 * GLA_QK]
    v = z[:, 2 * GLA_QK:2 * GLA_QK + MIX_WIDTH]
    gate = z[:, 2 * GLA_QK + MIX_WIDTH:2 * GLA_QK + 2 * MIX_WIDTH]
    a_low = z[:, 2 * GLA_QK + 2 * MIX_WIDTH:]
    log_alpha = _log_sigmoid(_mm(a_low, aw2_ref[...]) + ab_ref[...]) * (1.0 / GLA_TAU)
    o = _gla_block(q, k, v, log_alpha, st_ref, pm_ref, qk_ones_ref[...], chunk=GLA_CHUNK, dk=GLA_DK, dv=HEAD_DIM)
    o_ref[...] = _head_rmsnorm(o, ones_ref[...], nw_ref[...]) * (gate * _sigmoid(gate))


def _time_grid(T, seq, ts):
    n_t = seq // ts
    return ts, n_t, (T // seq, n_t)


def _hgrn2(z_hg, lb_table, norm_w, layer, seq):
    T = z_hg.shape[0]
    ts, n_t, grid = _time_grid(T, seq, GLA_TIME_BLOCK)
    blk = lambda w: pl.BlockSpec((ts, w), lambda b, i: (b * n_t + i, 0))
    pm = _prefix_matrix(GLA_CHUNK)
    ones = _block_ones(N_HEADS, HEAD_DIM, HEAD_DIM)
    return pl.pallas_call(
        functools.partial(_hgrn2_kernel, layer=layer),
        grid=grid,
        in_specs=[blk(HG_WIDTH), _const_spec(lb_table.shape), _const_spec((1, MIX_WIDTH)),
                  _const_spec(pm.shape), _const_spec(ones.shape)],
        out_specs=blk(MIX_WIDTH),
        out_shape=jax.ShapeDtypeStruct((T, MIX_WIDTH), F32),
        scratch_shapes=[pltpu.VMEM((N_HEADS, HEAD_DIM, HEAD_DIM), F32)],
        compiler_params=_params(2),
        name="hgrn2",
    )(z_hg, lb_table, norm_w, pm, ones)


def _gla(z_gla, a_w2, a_b, norm_w, seq):
    T = z_gla.shape[0]
    ts, n_t, grid = _time_grid(T, seq, GLA_TIME_BLOCK)
    blk = lambda w: pl.BlockSpec((ts, w), lambda b, i: (b * n_t + i, 0))
    pm = _prefix_matrix(GLA_CHUNK)
    qk_ones = _block_ones(N_HEADS, GLA_DK, HEAD_DIM)
    ones = _block_ones(N_HEADS, HEAD_DIM, HEAD_DIM)
    return pl.pallas_call(
        _gla_kernel,
        grid=grid,
        in_specs=[blk(GLA_WIDTH), _const_spec(a_w2.shape), _const_spec((1, GLA_QK)), _const_spec((1, MIX_WIDTH)),
                  _const_spec(pm.shape), _const_spec(qk_ones.shape), _const_spec(ones.shape)],
        out_specs=blk(MIX_WIDTH),
        out_shape=jax.ShapeDtypeStruct((T, MIX_WIDTH), F32),
        scratch_shapes=[pltpu.VMEM((N_HEADS, HEAD_DIM, GLA_DK), F32)],
        compiler_params=_params(2),
        name="gla",
    )(z_gla, a_w2, a_b, norm_w, pm, qk_ones, ones)


_RW_R, _RW_K, _RW_V = 0, MIX_WIDTH, 2 * MIX_WIDTH
_RW_WL = 3 * MIX_WIDTH
_RW_AL = _RW_WL + RW_DECAY_LORA
_RW_GL = _RW_AL + RW_AAA_LORA


def _rwkv_kernel(*refs, has_vfirst):
    if has_vfirst:
        (z_ref, vf_ref, mu_ref, w0_ref, w2_ref, a0_ref, a2_ref, g2_ref, kk_ref, ka_ref, rk_ref, lnw_ref, lnb_ref,
         v0_ref, v1_ref, v2_ref, tri_ref, ones_ref, o_ref, s_ref, carry_ref) = refs
    else:
        (z_ref, mu_ref, w0_ref, w2_ref, a0_ref, a2_ref, g2_ref, kk_ref, ka_ref, rk_ref, lnw_ref, lnb_ref,
         tri_ref, ones_ref, o_ref, vout_ref, s_ref, carry_ref) = refs
    C, HD = RW_CHUNK, HEAD_DIM
    ts = z_ref.shape[0]
    nc = ts // C
    pairs = [(c, h) for c in range(nc) for h in range(N_HEADS)]

    @pl.when(pl.program_id(1) == 0)
    def _():
        s_ref[...] = jnp.zeros_like(s_ref)
        carry_ref[...] = jnp.zeros_like(carry_ref)

    z = z_ref[...]
    zp = pltpu.roll(z, 1, 0)
    zp = jnp.where(lax.broadcasted_iota(jnp.int32, z.shape, 0) == 0, carry_ref[...], zp)
    carry_ref[...] = z[ts - 1:ts, :]
    zs = z + mu_ref[...] * (zp - z)
    r, k, v = zs[:, _RW_R:_RW_K], zs[:, _RW_K:_RW_V], zs[:, _RW_V:_RW_WL]
    wl, al, gl = zs[:, _RW_WL:_RW_AL], zs[:, _RW_AL:_RW_GL], zs[:, _RW_GL:]
    ones = ones_ref[...]
    gsum = lambda t: _mm_sel_rhs(t, ones)
    w = -_softplus(-(w0_ref[...] + _mm(jnp.tanh(wl), w2_ref[...]))) - 0.5
    lw = -jnp.exp(w)
    a = _sigmoid(a0_ref[...] + _mm(al, a2_ref[...]))
    g = _mm(_sigmoid(gl), g2_ref[...])
    if has_vfirst:
        v = v + (vf_ref[...] - v) * _sigmoid(v0_ref[...] + _mm(_mm(v, v1_ref[...]), v2_ref[...]))
    else:
        vout_ref[...] = v
    kk = k * kk_ref[...]
    kk = kk / jnp.maximum(jnp.sqrt(gsum(kk * kk)), 1e-12)
    k_mod = k * (1.0 + (a - 1.0) * ka_ref[...])
    kka = kk * a

    tri = tri_ref[...]
    cums = [_mm_sel_lhs(tri, lw[c * C:(c + 1) * C]) for c in range(nc)]
    cum = jnp.concatenate(cums, axis=0)
    cum_last = jnp.concatenate([jnp.broadcast_to(cc[C - 1:C, :], (C, MIX_WIDTH)) for cc in cums], axis=0)
    w_end = [jnp.exp(cc[C - 1:C, :]) for cc in cums]
    r_t = r * jnp.exp(cum)
    a_t = -kk * jnp.exp(cum - lw)
    inv = jnp.exp(-cum)
    end = jnp.exp(cum_last - cum)
    r_tb, a_tb = r_t.astype(BF16), a_t.astype(BF16)
    k_hb, b_hb = (k_mod * inv).astype(BF16), (kka * inv).astype(BF16)
    k_eb, b_eb = (k_mod * end).astype(BF16), (kka * end).astype(BF16)
    vb = v.astype(BF16)
    blk = lambda t, c, h: t[c * C:(c + 1) * C, h * HD:(h + 1) * HD]

    ti = lax.broadcasted_iota(jnp.int32, (C, C), 0)
    si = lax.broadcasted_iota(jnp.int32, (C, C), 1)
    incl, strict = si <= ti, si < ti
    dot = lambda x, y: jnp.dot(x, y, preferred_element_type=F32)
    aa = {p: _mm_nt(jnp.concatenate([blk(r_tb, *p), blk(a_tb, *p)], axis=0),
                    jnp.concatenate([blk(k_hb, *p), blk(b_hb, *p)], axis=0)) for p in pairs}
    incl2 = (lax.broadcasted_iota(jnp.int32, (C, 2 * C), 1) & (C - 1)) <= lax.broadcasted_iota(jnp.int32, (C, 2 * C), 0)
    a_r = {p: jnp.where(incl2, aa[p][:C], 0.0).astype(BF16) for p in pairs}
    a_ak = {p: jnp.where(strict, aa[p][C:, :C], 0.0).astype(BF16) for p in pairs}
    npow = {p: jnp.where(strict, aa[p][C:, C:], 0.0) for p in pairs}
    pv = {p: dot(a_ak[p], blk(vb, *p)) for p in pairs}
    x = {p: jnp.concatenate([blk(a_t, *p), pv[p]], axis=1) for p in pairs}
    n_stage = _n_levels(C)
    for j in range(n_stage):
        last = j == n_stage - 1
        nb = {p: npow[p].astype(BF16) for p in pairs}
        nx = {p: dot(nb[p], x[p].astype(BF16)) for p in pairs}
        if not last:
            npow = {p: dot(nb[p], nb[p]) for p in pairs}
        x = {p: x[p] + nx[p] for p in pairs}
    rmat = {p: jnp.concatenate([jnp.concatenate([jnp.zeros((C, HD), BF16), blk(vb, *p)], axis=1), x[p].astype(BF16)],
                               axis=0) for p in pairs}
    qz = {p: dot(a_r[p], rmat[p]) for p in pairs}
    gh = {p: lax.dot_general(rmat[p], jnp.concatenate([blk(k_eb, *p), blk(b_eb, *p)], axis=0),
                             (((0,), (0,)), ((), ())), preferred_element_type=F32) for p in pairs}
    q_f = {p: (blk(r_t, *p) + qz[p][:, :HD]).astype(BF16) for p in pairs}

    s = [s_ref[h] for h in range(N_HEADS)]
    y_rows = []
    for c in range(nc):
        sb = [s[h].astype(BF16) for h in range(N_HEADS)]
        ys = [lax.dot_general(q_f[c, h], sb[h], (((1,), (1,)), ((), ())), preferred_element_type=F32)
              + qz[c, h][:, HD:] for h in range(N_HEADS)]
        sg = [dot(sb[h], gh[c, h][:HD].astype(BF16)) for h in range(N_HEADS)]
        s = [s[h] * w_end[c][:, h * HD:(h + 1) * HD] + sg[h] + gh[c, h][HD:] for h in range(N_HEADS)]
        y_rows.append(jnp.concatenate(ys, axis=1))
    for h in range(N_HEADS):
        s_ref[h] = s[h]
    y = jnp.concatenate(y_rows, axis=0)

    mean = gsum(y) * (1.0 / HD)
    yc = y - mean
    var = gsum(yc * yc) * (1.0 / HD)
    yn = yc * lax.rsqrt(var + RW_GN_EPS) * lnw_ref[...] + lnb_ref[...]
    bonus = gsum(r * k_mod * rk_ref[...]) * v
    o_ref[...] = (yn + bonus) * g


def _rwkv(z_rw, v_first, p, seq):
    T = z_rw.shape[0]
    ts, n_t, grid = _time_grid(T, seq, RW_TIME_BLOCK)
    blk = lambda w: pl.BlockSpec((ts, w), lambda b, i: (b * n_t + i, 0))
    tri = jnp.asarray(np.tril(np.ones((RW_CHUNK, RW_CHUNK), np.float32)), BF16)
    ones = _block_ones(N_HEADS, HEAD_DIM, HEAD_DIM)
    has_vfirst = v_first is not None
    names = ["mu", "w0", "w2", "a0", "a2", "g2", "k_k", "k_a", "r_k", "ln_w", "ln_b"]
    args = [z_rw] + ([v_first] if has_vfirst else []) + [p[n] for n in names]
    specs = [blk(RW_WIDTH)] + ([blk(MIX_WIDTH)] if has_vfirst else []) + [_const_spec(p[n].shape) for n in names]
    if has_vfirst:
        args += [p["v0"], p["v1"], p["v2"]]
        specs += [_const_spec(p[n].shape) for n in ("v0", "v1", "v2")]
    args += [tri, ones]
    specs += [_const_spec(tri.shape), _const_spec(ones.shape)]
    out_sds = jax.ShapeDtypeStruct((T, MIX_WIDTH), F32)
    res = pl.pallas_call(
        functools.partial(_rwkv_kernel, has_vfirst=has_vfirst),
        grid=grid,
        in_specs=specs,
        out_specs=blk(MIX_WIDTH) if has_vfirst else [blk(MIX_WIDTH), blk(MIX_WIDTH)],
        out_shape=out_sds if has_vfirst else [out_sds, out_sds],
        scratch_shapes=[pltpu.VMEM((N_HEADS, HEAD_DIM, HEAD_DIM), F32), pltpu.VMEM((1, RW_WIDTH), F32)],
        compiler_params=_params(2),
        name="rwkv7",
    )(*args)
    return (res, None) if has_vfirst else (res[0], res[1])


ATT_TOKENS = ATT_BLOCK * max(d for _, d in DIL_PAIRS)
ATT_UNITS_PER_BATCH = 2


def _attn_kernel(c1_ref, p1_ref, c4_ref, p4_ref, c16_ref, p16_ref, o_ref, acc_scr, m_scr, l_scr):
    qb = ATT_BLOCK
    n_tile = MIX_WIDTH // LANES
    row = lax.broadcasted_iota(jnp.int32, (qb, 2 * qb), 0)
    col = lax.broadcasted_iota(jnp.int32, (qb, 2 * qb), 1)
    band = (col >= row) & (col <= row + qb)
    band_first = band & (col >= jnp.where(pl.program_id(1) > 0, 0, qb))
    low = lax.broadcasted_iota(jnp.int32, (qb, LANES), 1) < HEAD_DIM
    nt = (((1,), (1,)), ((), ()))
    curs, prevs = (c1_ref, c4_ref, c16_ref), (p1_ref, p4_ref, p16_ref)
    units = [(g, dil, r, qi) for g, (_, dil) in enumerate(DIL_PAIRS) for r in range(dil)
             for qi in range(ATT_TOKENS // dil // qb)]
    pairs = [(t, e) for t in range(n_tile) for e in range(LANES // HEAD_DIM)]
    tile = lambda x, t: x[:, t * LANES:(t + 1) * LANES]
    for b0 in range(0, len(units), ATT_UNITS_PER_BATCH):
        batch = units[b0:b0 + ATT_UNITS_PER_BATCH]
        vs, masks, scores = [], [], []
        for g, dil, r, qi in batch:
            cur, prev = curs[g], prevs[g]
            qc, kc, vc = (r * QKV_WIDTH + j * MIX_WIDTH for j in range(3))
            q = cur[0, qi * qb:(qi + 1) * qb, qc:qc + MIX_WIDTH]
            if qi == 0:
                k = jnp.concatenate([prev[0, :, kc:kc + MIX_WIDTH], cur[0, :qb, kc:kc + MIX_WIDTH]], axis=0)
                v = jnp.concatenate([prev[0, :, vc:vc + MIX_WIDTH], cur[0, :qb, vc:vc + MIX_WIDTH]], axis=0)
            else:
                k = cur[0, (qi - 1) * qb:(qi + 1) * qb, kc:kc + MIX_WIDTH]
                v = cur[0, (qi - 1) * qb:(qi + 1) * qb, vc:vc + MIX_WIDTH]
            vs.append(v)
            masks.append(band_first if qi == 0 else band)
            zero = jnp.zeros((qb, LANES), q.dtype)
            scores.append({(t, e): lax.dot_general(jnp.where(low if e == 0 else ~low, tile(q, t), zero), tile(k, t), nt,
                                                   preferred_element_type=F32) for t, e in pairs})
        probs, stats = [], []
        for u in range(len(batch)):
            pu, su = {}, {}
            for p in pairs:
                s = jnp.where(masks[u], scores[u][p], MASK_VALUE)
                m = jnp.max(s, axis=-1, keepdims=True)
                ex = jnp.exp2(s - m)
                su[p] = (m, jnp.sum(ex, axis=-1, keepdims=True))
                pu[p] = ex.astype(BF16)
            probs.append(pu)
            stats.append(su)
        pvs = [{(t, e): jnp.dot(probs[u][t, e], tile(vs[u], t), preferred_element_type=F32) for t, e in pairs}
               for u in range(len(batch))]
        for u, (g, dil, r, qi) in enumerate(batch):
            tok = pl.ds(qi * qb * dil + r, qb, stride=dil) if dil > 1 else pl.ds(qi * qb, qb)
            for t in range(n_tile):
                acc = jnp.where(low, pvs[u][t, 0], pvs[u][t, 1])
                m_new = jnp.where(low, stats[u][t, 0][0], stats[u][t, 1][0])
                l_new = jnp.where(low, stats[u][t, 0][1], stats[u][t, 1][1])
                if g > 0:
                    m_old = m_scr[t, tok, :]
                    m_tot = jnp.maximum(m_old, m_new)
                    e_old, e_new = jnp.exp2(m_old - m_tot), jnp.exp2(m_new - m_tot)
                    acc = acc_scr[t, tok, :] * e_old + acc * e_new
                    l_new = l_scr[t, tok, :] * e_old + l_new * e_new
                    m_new = m_tot
                acc_scr[t, tok, :] = acc
                m_scr[t, tok, :] = m_new
                l_scr[t, tok, :] = l_new
    for t in range(n_tile):
        o_ref[:, t * LANES:(t + 1) * LANES] = acc_scr[t] / l_scr[t]


def _attention(at, batch, seq):
    T = batch * seq
    nblk = seq // ATT_TOKENS
    args, specs = [], []
    for a, (_, d) in zip(at, DIL_PAIRS):
        rows = ATT_TOKENS // d
        n_prev = rows // ATT_BLOCK
        a3 = a.reshape(batch, seq // d, d * QKV_WIDTH)
        args += [a3, a3]
        specs += [pl.BlockSpec((1, rows, d * QKV_WIDTH), lambda b, i: (b, i, 0)),
                  pl.BlockSpec((1, ATT_BLOCK, d * QKV_WIDTH), lambda b, i, n=n_prev: (b, jnp.maximum(i * n - 1, 0), 0))]
    return pl.pallas_call(
        _attn_kernel,
        grid=(batch, nblk),
        in_specs=specs,
        out_specs=pl.BlockSpec((ATT_TOKENS, MIX_WIDTH), lambda b, i: (b * nblk + i, 0)),
        out_shape=jax.ShapeDtypeStruct((T, MIX_WIDTH), F32),
        scratch_shapes=[pltpu.VMEM((MIX_WIDTH // LANES, ATT_TOKENS, LANES), F32)] * 3,
        compiler_params=_params(2),
        name="dilated_attn",
    )(*args)


def _merge_kernel(x_ref, nw_ref, wg_ref, oa_ref, ob_ref, oc_ref, od_ref, wb_ref, wo_ref, out_ref):
    x = x_ref[...]
    hb = _rms(x, nw_ref[...]).astype(BF16)
    acc = jnp.zeros(x.shape, F32)
    for kbr, o in enumerate((oa_ref[...], ob_ref[...], oc_ref[...], od_ref[...])):
        gate = _sigmoid(jnp.dot(hb, wg_ref[:, kbr * D_MODEL:(kbr + 1) * D_MODEL], preferred_element_type=F32))
        acc = acc + gate * _mm(o, wb_ref[kbr])
    out_ref[...] = x + _mm(acc, wo_ref[...])


def _merge(x2, nw, w_gate, o_a, o_b, o_c, o_d, w_branch, w_out, layer):
    T, D = x2.shape
    tm = ROW_TILE
    row = lambda w: pl.BlockSpec((tm, w), lambda i: (i, 0))
    return pl.pallas_call(
        _merge_kernel,
        grid=(T // tm,),
        in_specs=[row(D), _const_spec((1, D)), _layer_spec(w_gate.shape, layer)] + [row(MIX_WIDTH)] * N_BRANCH
                 + [_layer_spec(w_branch.shape, layer), _layer_spec(w_out.shape, layer)],
        out_specs=row(D),
        out_shape=jax.ShapeDtypeStruct((T, D), F32),
        compiler_params=_params(1),
        name="merge",
    )(x2, nw, w_gate, o_a, o_b, o_c, o_d, w_branch, w_out)


def _ffn_kernel(x_ref, nw_ref, wup_ref, cw_ref, cb_ref, wdn_ref, fw_ref, out_ref, tail_ref, *, n_s, final_norm):
    tm = x_ref.shape[0]

    @pl.when(pl.program_id(0) % n_s == 0)
    def _():
        tail_ref[...] = jnp.zeros_like(tail_ref)

    x = x_ref[...]
    hb = _rms(x, nw_ref[...]).astype(BF16)
    up = jnp.dot(hb, wup_ref[:, :D_FF], preferred_element_type=F32)
    gate = jnp.dot(hb, wup_ref[:, D_FF:], preferred_element_type=F32)
    row = lax.broadcasted_iota(jnp.int32, up.shape, 0)
    tail = tail_ref[...]
    up1 = jnp.where(row == 0, tail[1:2, :], pltpu.roll(up, 1, 0))
    up2 = jnp.where(row == 0, tail[0:1, :], jnp.where(row == 1, tail[1:2, :], pltpu.roll(up, 2, 0)))
    tail_ref[...] = up[tm - 2:tm, :]
    cw = cw_ref[...]
    conv = cb_ref[...] + cw[0:1, :] * up + cw[1:2, :] * up1 + cw[2:3, :] * up2
    act = conv * _sigmoid(conv) * gate
    y = x + _mm(act, wdn_ref[...])
    out_ref[...] = _rms(y, fw_ref[...]) if final_norm else y


def _ffn(x2, nw, w_up, conv_w, conv_b, w_down, final_w, layer, seq, final_norm):
    T, D = x2.shape
    tm = ROW_TILE
    row = pl.BlockSpec((tm, D), lambda i: (i, 0))
    return pl.pallas_call(
        functools.partial(_ffn_kernel, n_s=seq // tm, final_norm=final_norm),
        grid=(T // tm,),
        in_specs=[row, _const_spec((1, D)), _layer_spec(w_up.shape, layer), _const_spec(conv_w.shape),
                  _const_spec((1, D_FF)), _layer_spec(w_down.shape, layer), _const_spec((1, D))],
        out_specs=row,
        out_shape=jax.ShapeDtypeStruct((T, D), F32),
        scratch_shapes=[pltpu.VMEM((CONV_W - 1, D_FF), F32)],
        compiler_params=_params(1),
        name="convglu",
    )(x2, nw, w_up, conv_w, conv_b, w_down, final_w)


def kernel(x, norm_mix_w, norm_ffn_w, norm_final_w, w_in, hg_lb_table, hg_norm_w, gla_a_w2, gla_a_b, gla_norm_w, rw_mu, rw_w0, rw_w2, rw_a0, rw_a2, rw_g2, rw_k_k, rw_k_a, rw_r_k, rw_ln_w, rw_ln_b, rw_v0, rw_v1, rw_v2, w_branch, w_out, ffn_w_up, ffn_conv_w, ffn_conv_b, ffn_w_down):
    B, S, D = x.shape
    T = B * S
    x2 = x.reshape(T, D)
    cos_t, sin_t = _rope_tables(S)
    row = lambda t: t.reshape(1, -1)
    W = MIX_WIDTH
    src = np.concatenate([np.arange(0, W), np.arange(W + RW_DECAY_LORA, 3 * W + RW_DECAY_LORA),
                          np.arange(W, W + RW_DECAY_LORA), np.arange(3 * W + RW_DECAY_LORA, RW_WIDTH)])
    rw_runs = [(0, W), (W + RW_DECAY_LORA, 3 * W + RW_DECAY_LORA), (W, W + RW_DECAY_LORA),
               (3 * W + RW_DECAY_LORA, RW_WIDTH)]
    at_runs = [(j * AT_WIDTH + g * W, j * AT_WIDTH + (g + 1) * W) for g in range(N_DIL) for j in range(3)]
    o_hg, o_gla, o_rw, o_at = 0, HG_WIDTH, HG_WIDTH + GLA_WIDTH, HG_WIDTH + GLA_WIDTH + RW_WIDTH
    o_gate = o_at + 3 * AT_WIDTH
    runs = ([(o_hg, o_rw)] + [None] + [(o_rw + a, o_rw + b) for a, b in rw_runs]
            + [(o_at + a, o_at + b) for a, b in at_runs])
    gap = jnp.zeros((DEPTH, D, _MIX_RW - _MIX_GLA - GLA_WIDTH), w_in.dtype)
    w_mix = jnp.concatenate([gap if r is None else w_in[:, :, r[0]:r[1]] for r in runs], axis=2).astype(BF16)
    w_gate = w_in[:, :, o_gate:].astype(BF16)
    w_branch_b, w_out_b = w_branch.astype(BF16), w_out.astype(BF16)
    w_up_b, w_down_b = ffn_w_up.astype(BF16), ffn_w_down.astype(BF16)
    v_first = None
    for layer in range(DEPTH):
        z_hg, z_gla, z_rw, *at = _inproj(x2, row(norm_mix_w[layer]), cos_t, sin_t, w_mix, layer, S)

        o_a = _hgrn2(z_hg, hg_lb_table, row(hg_norm_w[layer]), layer, S)
        o_b = _gla(z_gla, gla_a_w2[layer].astype(BF16), row(gla_a_b[layer]), row(gla_norm_w[layer]), S)
        p = {"mu": row(rw_mu[layer][src]), "w0": row(rw_w0[layer]), "w2": rw_w2[layer].astype(BF16),
             "a0": row(rw_a0[layer]), "a2": rw_a2[layer].astype(BF16), "g2": rw_g2[layer].astype(BF16),
             "k_k": row(rw_k_k[layer]), "k_a": row(rw_k_a[layer]), "r_k": row(rw_r_k[layer]),
             "ln_w": row(rw_ln_w[layer]), "ln_b": row(rw_ln_b[layer])}
        if layer > 0:
            p.update(v0=row(rw_v0[layer - 1]), v1=rw_v1[layer - 1].astype(BF16), v2=rw_v2[layer - 1].astype(BF16))
        o_c, v_new = _rwkv(z_rw, v_first, p, S)
        if layer == 0:
            v_first = v_new
        o_d = _attention(at, B, S)

        x2 = _merge(x2, row(norm_mix_w[layer]), w_gate, o_a, o_b, o_c, o_d, w_branch_b, w_out_b, layer)
        x2 = _ffn(x2, row(norm_ffn_w[layer]), w_up_b, ffn_conv_w[layer],
                  row(ffn_conv_b[layer]), w_down_b, row(norm_final_w), layer, S,
                  final_norm=(layer == DEPTH - 1))
    return x2.reshape(B, S, D)
```

```python
import functools

import numpy as np
import jax
import jax.numpy as jnp
from jax import lax
from jax.experimental import pallas as pl
from jax.experimental.pallas import tpu as pltpu

F32 = jnp.float32
BF16 = jnp.bfloat16

D_MODEL = 1024
DEPTH = 2
HEAD_DIM = 64
MIX_WIDTH = D_MODEL // 4
N_BRANCH = 4
NORM_EPS = 1e-6
MASK_VALUE = -1e9
EXP_CLIP = 60.0
N_HEADS = MIX_WIDTH // HEAD_DIM
GLA_DK = 32
GLA_RANK = 16
GLA_TAU = 16.0
RW_DECAY_LORA = 32
RW_AAA_LORA = 32
RW_MV_LORA = 32
RW_GATE_LORA = 64
RW_GN_EPS = 64e-5
DIL_PAIRS = ((128, 1), (512, 4), (2048, 16))
N_DIL = 3
AT_WIDTH = N_DIL * N_HEADS * HEAD_DIM
ROPE_THETA = 10000.0
D_FF = 11 * D_MODEL // 4
CONV_W = 3

HG_WIDTH = 4 * MIX_WIDTH
GLA_QK = N_HEADS * GLA_DK
GLA_WIDTH = 2 * GLA_QK + 2 * MIX_WIDTH + GLA_RANK
RW_WIDTH = 3 * MIX_WIDTH + RW_DECAY_LORA + RW_AAA_LORA + RW_GATE_LORA
GATE_WIDTH = N_BRANCH * D_MODEL

LANES = 128
LOG2_E = 1.4426950408889634
VMEM_LIMIT_BYTES = 56 * 1024 * 1024
ROW_TILE = 512
GLA_TIME_BLOCK = 1024
RW_TIME_BLOCK = 512
GLA_CHUNK = 128
RW_CHUNK = 64
ATT_BLOCK = 128


def _const_spec(shape):
    nd = len(shape)
    return pl.BlockSpec(shape, lambda *_: (0,) * nd, pipeline_mode=pl.Buffered(1))


def _layer_spec(shape, layer):
    nd = len(shape)
    return pl.BlockSpec((None,) + tuple(shape[1:]), lambda *_: (layer,) + (0,) * (nd - 1), pipeline_mode=pl.Buffered(1))


def _params(n_grid):
    return pltpu.CompilerParams(dimension_semantics=("arbitrary",) * n_grid, vmem_limit_bytes=VMEM_LIMIT_BYTES)


def _mm(a, b):
    return jnp.dot(a.astype(BF16), b.astype(BF16), preferred_element_type=F32)


def _mm_nt(a, b):
    return lax.dot_general(a.astype(BF16), b.astype(BF16), (((1,), (1,)), ((), ())), preferred_element_type=F32)


def _mm_tn(a, b):
    return lax.dot_general(a.astype(BF16), b.astype(BF16), (((0,), (0,)), ((), ())), preferred_element_type=F32)


def _split2(a):
    hi = a.astype(BF16)
    lo = (a - hi.astype(F32)).astype(BF16)
    return hi, lo


def _mm_sel_lhs(sel, a):
    hi, lo = _split2(a)
    return jnp.dot(sel, hi, preferred_element_type=F32) + jnp.dot(sel, lo, preferred_element_type=F32)


def _mm_sel_rhs(a, sel):
    hi, lo = _split2(a)
    return jnp.dot(hi, sel, preferred_element_type=F32) + jnp.dot(lo, sel, preferred_element_type=F32)


def _rms(x, w):
    return x * lax.rsqrt(jnp.mean(x * x, axis=-1, keepdims=True) + NORM_EPS) * w


def _sigmoid(x):
    return jax.nn.sigmoid(x)


def _log_sigmoid(x):
    return jnp.minimum(x, 0.0) - jnp.log1p(jnp.exp(-jnp.abs(x)))


def _softplus(x):
    return jnp.maximum(x, 0.0) + jnp.log1p(jnp.exp(-jnp.abs(x)))


def _n_levels(chunk):
    return int(np.log2(chunk))


def _prefix_matrix(chunk):
    C, L = chunk, _n_levels(chunk)
    M = np.zeros(((L + 1) * C, C), np.float32)
    M[:C] = np.tril(np.ones((C, C), np.float32))
    for l in range(L):
        m = C >> (l + 1)
        for t in range(C):
            mid = (t // (2 * m)) * 2 * m + m - 1
            if t % (2 * m) >= m:
                M[(l + 1) * C + t, mid + 1:t + 1] = 1.0
            else:
                M[(l + 1) * C + t, t + 1:mid + 1] = 1.0
    return jnp.asarray(M, BF16)


def _block_ones(n_groups, w_in, w_out):
    return jnp.asarray(np.kron(np.eye(n_groups, dtype=np.float32), np.ones((w_in, w_out), np.float32)), BF16)


def _rope_tables(seq):
    half = HEAD_DIM // 2
    inv = ROPE_THETA ** (-np.arange(half, dtype=np.float64) / half)
    ang = np.arange(seq, dtype=np.float64)[:, None] * inv[None, :]
    cos, sin = np.cos(ang), np.sin(ang)
    cos_h = np.concatenate([cos, cos], axis=1)
    sin_h = np.concatenate([-sin, sin], axis=1)
    return (jnp.asarray(np.tile(cos_h, (1, N_HEADS)), F32), jnp.asarray(np.tile(sin_h, (1, N_HEADS)), F32))


QKV_WIDTH = 3 * MIX_WIDTH


_MIX_HG = 0
_MIX_GLA = _MIX_HG + HG_WIDTH
_MIX_RW = _MIX_GLA + -(-GLA_WIDTH // LANES) * LANES
_MIX_AT = _MIX_RW + RW_WIDTH
MIX_PROJ_WIDTH = _MIX_AT + 3 * AT_WIDTH


def _inproj_kernel(x_ref, nw_ref, cos_ref, sin_ref, w_ref,
                   zhg_ref, zgla_ref, zrw_ref, at1_ref, at4_ref, at16_ref, at_scr):
    tm = x_ref.shape[0]
    hb = _rms(x_ref[...], nw_ref[...]).astype(BF16)
    zat = jnp.dot(hb, w_ref[:, _MIX_AT:], preferred_element_type=F32)
    cos, sin = cos_ref[...], sin_ref[...]
    lane = lax.broadcasted_iota(jnp.int32, cos.shape, 1)
    first_half = (lane % HEAD_DIM) < (HEAD_DIM // 2)

    def rope(p):
        partner = jnp.where(first_half, pltpu.roll(p, MIX_WIDTH - HEAD_DIM // 2, 1), pltpu.roll(p, HEAD_DIM // 2, 1))
        return p * cos + partner * sin

    n_lane_blk = QKV_WIDTH // LANES
    for g, (out_ref, (_, dil)) in enumerate(zip((at1_ref, at4_ref, at16_ref), DIL_PAIRS)):
        c0 = g * QKV_WIDTH
        q = rope(zat[:, c0:c0 + MIX_WIDTH]) * (HEAD_DIM ** -0.5 * LOG2_E)
        k = rope(zat[:, c0 + MIX_WIDTH:c0 + 2 * MIX_WIDTH])
        qkv = jnp.concatenate([q, k, zat[:, c0 + 2 * MIX_WIDTH:c0 + QKV_WIDTH]], axis=1)
        if dil == 1:
            out_ref[...] = qkv.astype(BF16)
            continue
        for c in range(n_lane_blk):
            at_scr[c] = qkv[:, c * LANES:(c + 1) * LANES]
        for r in range(dil):
            rows = jnp.concatenate([at_scr[c, pl.ds(r, tm // dil, stride=dil), :] for c in range(n_lane_blk)], axis=1)
            out_ref[:, r * QKV_WIDTH:(r + 1) * QKV_WIDTH] = rows.astype(BF16)
    zhg_ref[...] = jnp.dot(hb, w_ref[:, _MIX_HG:_MIX_HG + HG_WIDTH], preferred_element_type=F32)
    zgla_ref[...] = jnp.dot(hb, w_ref[:, _MIX_GLA:_MIX_GLA + GLA_WIDTH], preferred_element_type=F32)
    zrw_ref[...] = jnp.dot(hb, w_ref[:, _MIX_RW:_MIX_RW + RW_WIDTH], preferred_element_type=F32)


def _inproj(x2, nw, cos_t, sin_t, w_mix, layer, seq):
    T, D = x2.shape
    tm = ROW_TILE
    n_s = seq // tm
    row = lambda w: pl.BlockSpec((tm, w), lambda i: (i, 0))
    pos = pl.BlockSpec((tm, MIX_WIDTH), lambda i: (i % n_s, 0))
    widths = (HG_WIDTH, GLA_WIDTH, RW_WIDTH)
    dils = [d for _, d in DIL_PAIRS]
    return pl.pallas_call(
        _inproj_kernel,
        grid=(T // tm,),
        in_specs=[row(D), _const_spec((1, D)), pos, pos, _layer_spec(w_mix.shape, layer)],
        out_specs=[row(w) for w in widths]
                  + [pl.BlockSpec((tm // d, d * QKV_WIDTH), lambda i: (i, 0)) for d in dils],
        out_shape=[jax.ShapeDtypeStruct((T, w), F32) for w in widths]
                  + [jax.ShapeDtypeStruct((T // d, d * QKV_WIDTH), BF16) for d in dils],
        scratch_shapes=[pltpu.VMEM((QKV_WIDTH // LANES, tm, LANES), F32)],
        compiler_params=_params(1),
        name="inproj",
    )(x2, nw, cos_t, sin_t, w_mix)


def _level_masks(chunk):
    t = lax.broadcasted_iota(jnp.int32, (chunk, chunk), 0)
    s = lax.broadcasted_iota(jnp.int32, (chunk, chunk), 1)
    masks = []
    for l in range(_n_levels(chunk)):
        m = chunk >> (l + 1)
        sh = int(np.log2(2 * m))
        masks.append(((t >> sh) == (s >> sh)) & ((t & m) != 0) & ((s & m) == 0))
    return masks


def _gla_block(q, k, v, g, st_ref, pm_ref, qk_ones, *, chunk, dk, dv):
    C = chunk
    nc = q.shape[0] // C
    n_lvl = _n_levels(C)
    masks = _level_masks(C)
    pm = pm_ref[...]
    rows = lambda t, c: t[c * C:(c + 1) * C]
    stack = lambda f: jnp.concatenate([f(c) for c in range(nc)], axis=0)
    g2 = g * LOG2_E
    D = [_mm_sel_lhs(pm, rows(g2, c)) for c in range(nc)]
    b = stack(lambda c: D[c][:C])
    b_last = stack(lambda c: jnp.broadcast_to(D[c][C - 1:C, :], (C, b.shape[1])))
    dec_last = [jnp.exp2(D[c][C - 1:C, :]) for c in range(nc)]
    q_in = (q * jnp.exp2(b)).astype(BF16)
    k_dec = (k * jnp.exp2(b_last - b)).astype(BF16)
    vb = v.astype(BF16)
    qf, kf = [], []
    for l in range(n_lvl):
        e = jnp.exp2(stack(lambda c: D[c][(l + 1) * C:(l + 2) * C]))
        qf.append((q * e).astype(BF16))
        kf.append((k * e).astype(BF16))
    diag = _mm(q * k, qk_ones)
    pairs = [(c, h) for c in range(nc) for h in range(N_HEADS)]
    bk = lambda t, c, h: t[c * C:(c + 1) * C, h * dk:(h + 1) * dk]
    bv = lambda t, c, h: t[c * C:(c + 1) * C, h * dv:(h + 1) * dv]
    nt = (((1,), (1,)), ((), ()))
    tn = (((0,), (0,)), ((), ()))
    prod = {(p, l): lax.dot_general(bk(qf[l], *p), bk(kf[l], *p), nt, preferred_element_type=F32)
            for p in pairs for l in range(n_lvl)}
    o_intra, kv = {}, {}
    for p in pairs:
        sc = jnp.zeros((C, C), F32)
        for l in range(n_lvl):
            sc = jnp.where(masks[l], prod[p, l], sc)
        o_intra[p] = jnp.dot(sc.astype(BF16), bv(vb, *p), preferred_element_type=F32)
        kv[p] = lax.dot_general(bv(vb, *p), bk(k_dec, *p), tn, preferred_element_type=F32)
    s = [st_ref[h] for h in range(N_HEADS)]
    out_rows = []
    for c in range(nc):
        o_inter = [lax.dot_general(bk(q_in, c, h), s[h].astype(BF16), nt, preferred_element_type=F32)
                   for h in range(N_HEADS)]
        s = [s[h] * dec_last[c][:, h * dk:(h + 1) * dk] + kv[c, h] for h in range(N_HEADS)]
        out_rows.append(jnp.concatenate([o_intra[c, h] + o_inter[h] for h in range(N_HEADS)], axis=1))
    for h in range(N_HEADS):
        st_ref[h] = s[h]
    return jnp.concatenate(out_rows, axis=0) + diag * v


def _head_rmsnorm(o, head_ones, w):
    ms = _mm(o * o, head_ones) * (1.0 / HEAD_DIM)
    return o * lax.rsqrt(ms + NORM_EPS) * w


def _hgrn2_kernel(z_ref, lbt_ref, nw_ref, pm_ref, ones_ref, o_ref, st_ref, *, layer):
    @pl.when(pl.program_id(1) == 0)
    def _():
        st_ref[...] = jnp.zeros_like(st_ref)

    tab = lbt_ref[...]
    e = jnp.exp(tab - jnp.max(tab, axis=0, keepdims=True))
    p = e / jnp.sum(e, axis=0, keepdims=True)
    lb = jnp.zeros((1, MIX_WIDTH), F32)
    for i in range(1, layer + 1):
        lb = lb + p[i:i + 1, :]
    ones = ones_ref[...]
    z = z_ref[...]
    q_raw, fz = z[:, :MIX_WIDTH], z[:, MIX_WIDTH:2 * MIX_WIDTH]
    i_raw, g_raw = z[:, 2 * MIX_WIDTH:3 * MIX_WIDTH], z[:, 3 * MIX_WIDTH:]
    log_f = _log_sigmoid(fz) + jnp.log1p(lb * jnp.exp(jnp.minimum(-fz, EXP_CLIP)))
    k = (1.0 - lb) * _sigmoid(-fz)
    q = q_raw * _sigmoid(q_raw)
    o = _gla_block(q, k, i_raw, log_f, st_ref, pm_ref, ones, chunk=GLA_CHUNK, dk=HEAD_DIM, dv=HEAD_DIM)
    o_ref[...] = _head_rmsnorm(o, ones, nw_ref[...]) * _sigmoid(g_raw)


def _gla_kernel(z_ref, aw2_ref, ab_ref, nw_ref, pm_ref, qk_ones_ref, ones_ref, o_ref, st_ref):
    @pl.when(pl.program_id(1) == 0)
    def _():
        st_ref[...] = jnp.zeros_like(st_ref)

    z = z_ref[...]
    q = z[:, :GLA_QK] * (GLA_DK ** -0.5)
    k = z[:, GLA_QK:2 * GLA_QK]
    v = z[:, 2 * GLA_QK:2 * GLA_QK + MIX_WIDTH]
    gate = z[:, 2 * GLA_QK + MIX_WIDTH:2 * GLA_QK + 2 * MIX_WIDTH]
    a_low = z[:, 2 * GLA_QK + 2 * MIX_WIDTH:]
    log_alpha = _log_sigmoid(_mm(a_low, aw2_ref[...]) + ab_ref[...]) * (1.0 / GLA_TAU)
    o = _gla_block(q, k, v, log_alpha, st_ref, pm_ref, qk_ones_ref[...], chunk=GLA_CHUNK, dk=GLA_DK, dv=HEAD_DIM)
    o_ref[...] = _head_rmsnorm(o, ones_ref[...], nw_ref[...]) * (gate * _sigmoid(gate))


def _time_grid(T, seq, ts):
    n_t = seq // ts
    return ts, n_t, (T // seq, n_t)


def _hgrn2(z_hg, lb_table, norm_w, layer, seq):
    T = z_hg.shape[0]
    ts, n_t, grid = _time_grid(T, seq, GLA_TIME_BLOCK)
    blk = lambda w: pl.BlockSpec((ts, w), lambda b, i: (b * n_t + i, 0))
    pm = _prefix_matrix(GLA_CHUNK)
    ones = _block_ones(N_HEADS, HEAD_DIM, HEAD_DIM)
    return pl.pallas_call(
        functools.partial(_hgrn2_kernel, layer=layer),
        grid=grid,
        in_specs=[blk(HG_WIDTH), _const_spec(lb_table.shape), _const_spec((1, MIX_WIDTH)),
                  _const_spec(pm.shape), _const_spec(ones.shape)],
        out_specs=blk(MIX_WIDTH),
        out_shape=jax.ShapeDtypeStruct((T, MIX_WIDTH), F32),
        scratch_shapes=[pltpu.VMEM((N_HEADS, HEAD_DIM, HEAD_DIM), F32)],
        compiler_params=_params(2),
        name="hgrn2",
    )(z_hg, lb_table, norm_w, pm, ones)


def _gla(z_gla, a_w2, a_b, norm_w, seq):
    T = z_gla.shape[0]
    ts, n_t, grid = _time_grid(T, seq, GLA_TIME_BLOCK)
    blk = lambda w: pl.BlockSpec((ts, w), lambda b, i: (b * n_t + i, 0))
    pm = _prefix_matrix(GLA_CHUNK)
    qk_ones = _block_ones(N_HEADS, GLA_DK, HEAD_DIM)
    ones = _block_ones(N_HEADS, HEAD_DIM, HEAD_DIM)
    return pl.pallas_call(
        _gla_kernel,
        grid=grid,
        in_specs=[blk(GLA_WIDTH), _const_spec(a_w2.shape), _const_spec((1, GLA_QK)), _const_spec((1, MIX_WIDTH)),
                  _const_spec(pm.shape), _const_spec(qk_ones.shape), _const_spec(ones.shape)],
        out_specs=blk(MIX_WIDTH),
        out_shape=jax.ShapeDtypeStruct((T, MIX_WIDTH), F32),
        scratch_shapes=[pltpu.VMEM((N_HEADS, HEAD_DIM, GLA_DK), F32)],
        compiler_params=_params(2),
        name="gla",
    )(z_gla, a_w2, a_b, norm_w, pm, qk_ones, ones)


_RW_R, _RW_K, _RW_V = 0, MIX_WIDTH, 2 * MIX_WIDTH
_RW_WL = 3 * MIX_WIDTH
_RW_AL = _RW_WL + RW_DECAY_LORA
_RW_GL = _RW_AL + RW_AAA_LORA


def _rwkv_kernel(*refs, has_vfirst):
    if has_vfirst:
        (z_ref, vf_ref, mu_ref, w0_ref, w2_ref, a0_ref, a2_ref, g2_ref, kk_ref, ka_ref, rk_ref, lnw_ref, lnb_ref,
         v0_ref, v1_ref, v2_ref, tri_ref, ones_ref, o_ref, s_ref, carry_ref) = refs
    else:
        (z_ref, mu_ref, w0_ref, w2_ref, a0_ref, a2_ref, g2_ref, kk_ref, ka_ref, rk_ref, lnw_ref, lnb_ref,
         tri_ref, ones_ref, o_ref, vout_ref, s_ref, carry_ref) = refs
    C, HD = RW_CHUNK, HEAD_DIM
    ts = z_ref.shape[0]
    nc = ts // C
    pairs = [(c, h) for c in range(nc) for h in range(N_HEADS)]

    @pl.when(pl.program_id(1) == 0)
    def _():
        s_ref[...] = jnp.zeros_like(s_ref)
        carry_ref[...] = jnp.zeros_like(carry_ref)

    z = z_ref[...]
    zp = pltpu.roll(z, 1, 0)
    zp = jnp.where(lax.broadcasted_iota(jnp.int32, z.shape, 0) == 0, carry_ref[...], zp)
    carry_ref[...] = z[ts - 1:ts, :]
    zs = z + mu_ref[...] * (zp - z)
    r, k, v = zs[:, _RW_R:_RW_K], zs[:, _RW_K:_RW_V], zs[:, _RW_V:_RW_WL]
    wl, al, gl = zs[:, _RW_WL:_RW_AL], zs[:, _RW_AL:_RW_GL], zs[:, _RW_GL:]
    ones = ones_ref[...]
    gsum = lambda t: _mm_sel_rhs(t, ones)
    w = -_softplus(-(w0_ref[...] + _mm(jnp.tanh(wl), w2_ref[...]))) - 0.5
    lw = -jnp.exp(w)
    a = _sigmoid(a0_ref[...] + _mm(al, a2_ref[...]))
    g = _mm(_sigmoid(gl), g2_ref[...])
    if has_vfirst:
        v = v + (vf_ref[...] - v) * _sigmoid(v0_ref[...] + _mm(_mm(v, v1_ref[...]), v2_ref[...]))
    else:
        vout_ref[...] = v
    kk = k * kk_ref[...]
    kk = kk / jnp.maximum(jnp.sqrt(gsum(kk * kk)), 1e-12)
    k_mod = k * (1.0 + (a - 1.0) * ka_ref[...])
    kka = kk * a

    tri = tri_ref[...]
    cums = [_mm_sel_lhs(tri, lw[c * C:(c + 1) * C]) for c in range(nc)]
    cum = jnp.concatenate(cums, axis=0)
    cum_last = jnp.concatenate([jnp.broadcast_to(cc[C - 1:C, :], (C, MIX_WIDTH)) for cc in cums], axis=0)
    w_end = [jnp.exp(cc[C - 1:C, :]) for cc in cums]
    r_t = r * jnp.exp(cum)
    a_t = -kk * jnp.exp(cum - lw)
    inv = jnp.exp(-cum)
    end = jnp.exp(cum_last - cum)
    r_tb, a_tb = r_t.astype(BF16), a_t.astype(BF16)
    k_hb, b_hb = (k_mod * inv).astype(BF16), (kka * inv).astype(BF16)
    k_eb, b_eb = (k_mod * end).astype(BF16), (kka * end).astype(BF16)
    vb = v.astype(BF16)
    blk = lambda t, c, h: t[c * C:(c + 1) * C, h * HD:(h + 1) * HD]

    ti = lax.broadcasted_iota(jnp.int32, (C, C), 0)
    si = lax.broadcasted_iota(jnp.int32, (C, C), 1)
    incl, strict = si <= ti, si < ti
    dot = lambda x, y: jnp.dot(x, y, preferred_element_type=F32)
    aa = {p: _mm_nt(jnp.concatenate([blk(r_tb, *p), blk(a_tb, *p)], axis=0),
                    jnp.concatenate([blk(k_hb, *p), blk(b_hb, *p)], axis=0)) for p in pairs}
    incl2 = (lax.broadcasted_iota(jnp.int32, (C, 2 * C), 1) & (C - 1)) <= lax.broadcasted_iota(jnp.int32, (C, 2 * C), 0)
    a_r = {p: jnp.where(incl2, aa[p][:C], 0.0).astype(BF16) for p in pairs}
    a_ak = {p: jnp.where(strict, aa[p][C:, :C], 0.0).astype(BF16) for p in pairs}
    npow = {p: jnp.where(strict, aa[p][C:, C:], 0.0) for p in pairs}
    pv = {p: dot(a_ak[p], blk(vb, *p)) for p in pairs}
    x = {p: jnp.concatenate([blk(a_t, *p), pv[p]], axis=1) for p in pairs}
    n_stage = _n_levels(C)
    for j in range(n_stage):
        last = j == n_stage - 1
        nb = {p: npow[p].astype(BF16) for p in pairs}
        nx = {p: dot(nb[p], x[p].astype(BF16)) for p in pairs}
        if not last:
            npow = {p: dot(nb[p], nb[p]) for p in pairs}
        x = {p: x[p] + nx[p] for p in pairs}
    rmat = {p: jnp.concatenate([jnp.concatenate([jnp.zeros((C, HD), BF16), blk(vb, *p)], axis=1), x[p].astype(BF16)],
                               axis=0) for p in pairs}
    qz = {p: dot(a_r[p], rmat[p]) for p in pairs}
    gh = {p: lax.dot_general(rmat[p], jnp.concatenate([blk(k_eb, *p), blk(b_eb, *p)], axis=0),
                             (((0,), (0,)), ((), ())), preferred_element_type=F32) for p in pairs}
    q_f = {p: (blk(r_t, *p) + qz[p][:, :HD]).astype(BF16) for p in pairs}

    s = [s_ref[h] for h in range(N_HEADS)]
    y_rows = []
    for c in range(nc):
        sb = [s[h].astype(BF16) for h in range(N_HEADS)]
        ys = [lax.dot_general(q_f[c, h], sb[h], (((1,), (1,)), ((), ())), preferred_element_type=F32)
              + qz[c, h][:, HD:] for h in range(N_HEADS)]
        sg = [dot(sb[h], gh[c, h][:HD].astype(BF16)) for h in range(N_HEADS)]
        s = [s[h] * w_end[c][:, h * HD:(h + 1) * HD] + sg[h] + gh[c, h][HD:] for h in range(N_HEADS)]
        y_rows.append(jnp.concatenate(ys, axis=1))
    for h in range(N_HEADS):
        s_ref[h] = s[h]
    y = jnp.concatenate(y_rows, axis=0)

    hsum = lambda t: _mm(t, ones)
    mean = hsum(y) * (1.0 / HD)
    yc = y - mean
    var = hsum(yc * yc) * (1.0 / HD)
    yn = yc * lax.rsqrt(var + RW_GN_EPS) * lnw_ref[...] + lnb_ref[...]
    bonus = hsum(r * k_mod * rk_ref[...]) * v
    o_ref[...] = (yn + bonus) * g


def _rwkv(z_rw, v_first, p, seq):
    T = z_rw.shape[0]
    ts, n_t, grid = _time_grid(T, seq, RW_TIME_BLOCK)
    blk = lambda w: pl.BlockSpec((ts, w), lambda b, i: (b * n_t + i, 0))
    tri = jnp.asarray(np.tril(np.ones((RW_CHUNK, RW_CHUNK), np.float32)), BF16)
    ones = _block_ones(N_HEADS, HEAD_DIM, HEAD_DIM)
    has_vfirst = v_first is not None
    names = ["mu", "w0", "w2", "a0", "a2", "g2", "k_k", "k_a", "r_k", "ln_w", "ln_b"]
    args = [z_rw] + ([v_first] if has_vfirst else []) + [p[n] for n in names]
    specs = [blk(RW_WIDTH)] + ([blk(MIX_WIDTH)] if has_vfirst else []) + [_const_spec(p[n].shape) for n in names]
    if has_vfirst:
        args += [p["v0"], p["v1"], p["v2"]]
        specs += [_const_spec(p[n].shape) for n in ("v0", "v1", "v2")]
    args += [tri, ones]
    specs += [_const_spec(tri.shape), _const_spec(ones.shape)]
    out_sds = jax.ShapeDtypeStruct((T, MIX_WIDTH), F32)
    res = pl.pallas_call(
        functools.partial(_rwkv_kernel, has_vfirst=has_vfirst),
        grid=grid,
        in_specs=specs,
        out_specs=blk(MIX_WIDTH) if has_vfirst else [blk(MIX_WIDTH), blk(MIX_WIDTH)],
        out_shape=out_sds if has_vfirst else [out_sds, out_sds],
        scratch_shapes=[pltpu.VMEM((N_HEADS, HEAD_DIM, HEAD_DIM), F32), pltpu.VMEM((1, RW_WIDTH), F32)],
        compiler_params=_params(2),
        name="rwkv7",
    )(*args)
    return (res, None) if has_vfirst else (res[0], res[1])


ATT_TOKENS = ATT_BLOCK * max(d for _, d in DIL_PAIRS)
ATT_UNITS_PER_BATCH = 2


def _attn_kernel(c1_ref, p1_ref, c4_ref, p4_ref, c16_ref, p16_ref, o_ref, acc_scr, m_scr, l_scr):
    qb = ATT_BLOCK
    n_tile = MIX_WIDTH // LANES
    row = lax.broadcasted_iota(jnp.int32, (qb, 2 * qb), 0)
    col = lax.broadcasted_iota(jnp.int32, (qb, 2 * qb), 1)
    band = (col >= row) & (col <= row + qb)
    band_first = band & (col >= jnp.where(pl.program_id(1) > 0, 0, qb))
    low = lax.broadcasted_iota(jnp.int32, (qb, LANES), 1) < HEAD_DIM
    nt = (((1,), (1,)), ((), ()))
    curs, prevs = (c1_ref, c4_ref, c16_ref), (p1_ref, p4_ref, p16_ref)
    units = [(g, dil, r, qi) for g, (_, dil) in enumerate(DIL_PAIRS) for r in range(dil)
             for qi in range(ATT_TOKENS // dil // qb)]
    pairs = [(t, e) for t in range(n_tile) for e in range(LANES // HEAD_DIM)]
    tile = lambda x, t: x[:, t * LANES:(t + 1) * LANES]
    for b0 in range(0, len(units), ATT_UNITS_PER_BATCH):
        batch = units[b0:b0 + ATT_UNITS_PER_BATCH]
        vs, masks, scores = [], [], []
        for g, dil, r, qi in batch:
            cur, prev = curs[g], prevs[g]
            qc, kc, vc = (r * QKV_WIDTH + j * MIX_WIDTH for j in range(3))
            q = cur[0, qi * qb:(qi + 1) * qb, qc:qc + MIX_WIDTH]
            if qi == 0:
                k = jnp.concatenate([prev[0, :, kc:kc + MIX_WIDTH], cur[0, :qb, kc:kc + MIX_WIDTH]], axis=0)
                v = jnp.concatenate([prev[0, :, vc:vc + MIX_WIDTH], cur[0, :qb, vc:vc + MIX_WIDTH]], axis=0)
            else:
                k = cur[0, (qi - 1) * qb:(qi + 1) * qb, kc:kc + MIX_WIDTH]
                v = cur[0, (qi - 1) * qb:(qi + 1) * qb, vc:vc + MIX_WIDTH]
            vs.append(v)
            masks.append(band_first if qi == 0 else band)
            zero = jnp.zeros((qb, LANES), q.dtype)
            scores.append({(t, e): lax.dot_general(jnp.where(low if e == 0 else ~low, tile(q, t), zero), tile(k, t), nt,
                                                   preferred_element_type=F32) for t, e in pairs})
        probs, stats = [], []
        for u in range(len(batch)):
            pu, su = {}, {}
            for p in pairs:
                s = jnp.where(masks[u], scores[u][p], MASK_VALUE)
                m = jnp.max(s, axis=-1, keepdims=True)
                ex = jnp.exp2(s - m)
                su[p] = (m, jnp.sum(ex, axis=-1, keepdims=True))
                pu[p] = ex.astype(BF16)
            probs.append(pu)
            stats.append(su)
        pvs = [{(t, e): jnp.dot(probs[u][t, e], tile(vs[u], t), preferred_element_type=F32) for t, e in pairs}
               for u in range(len(batch))]
        for u, (g, dil, r, qi) in enumerate(batch):
            tok = pl.ds(qi * qb * dil + r, qb, stride=dil) if dil > 1 else pl.ds(qi * qb, qb)
            for t in range(n_tile):
                acc = jnp.where(low, pvs[u][t, 0], pvs[u][t, 1])
                m_new = jnp.where(low, stats[u][t, 0][0], stats[u][t, 1][0])
                l_new = jnp.where(low, stats[u][t, 0][1], stats[u][t, 1][1])
                if g > 0:
                    m_old = m_scr[t, tok, :]
                    m_tot = jnp.maximum(m_old, m_new)
                    e_old, e_new = jnp.exp2(m_old - m_tot), jnp.exp2(m_new - m_tot)
                    acc = acc_scr[t, tok, :] * e_old + acc * e_new
                    l_new = l_scr[t, tok, :] * e_old + l_new * e_new
                    m_new = m_tot
                acc_scr[t, tok, :] = acc
                m_scr[t, tok, :] = m_new
                l_scr[t, tok, :] = l_new
    for t in range(n_tile):
        o_ref[:, t * LANES:(t + 1) * LANES] = acc_scr[t] / l_scr[t]


def _attention(at, batch, seq):
    T = batch * seq
    nblk = seq // ATT_TOKENS
    args, specs = [], []
    for a, (_, d) in zip(at, DIL_PAIRS):
        rows = ATT_TOKENS // d
        n_prev = rows // ATT_BLOCK
        a3 = a.reshape(batch, seq // d, d * QKV_WIDTH)
        args += [a3, a3]
        specs += [pl.BlockSpec((1, rows, d * QKV_WIDTH), lambda b, i: (b, i, 0)),
                  pl.BlockSpec((1, ATT_BLOCK, d * QKV_WIDTH), lambda b, i, n=n_prev: (b, jnp.maximum(i * n - 1, 0), 0))]
    return pl.pallas_call(
        _attn_kernel,
        grid=(batch, nblk),
        in_specs=specs,
        out_specs=pl.BlockSpec((ATT_TOKENS, MIX_WIDTH), lambda b, i: (b * nblk + i, 0)),
        out_shape=jax.ShapeDtypeStruct((T, MIX_WIDTH), F32),
        scratch_shapes=[pltpu.VMEM((MIX_WIDTH // LANES, ATT_TOKENS, LANES), F32)] * 3,
        compiler_params=_params(2),
        name="dilated_attn",
    )(*args)


def _merge_kernel(x_ref, nw_ref, wg_ref, oa_ref, ob_ref, oc_ref, od_ref, wb_ref, wo_ref, out_ref):
    x = x_ref[...]
    hb = _rms(x, nw_ref[...]).astype(BF16)
    acc = jnp.zeros(x.shape, F32)
    for kbr, o in enumerate((oa_ref[...], ob_ref[...], oc_ref[...], od_ref[...])):
        gate = _sigmoid(jnp.dot(hb, wg_ref[:, kbr * D_MODEL:(kbr + 1) * D_MODEL], preferred_element_type=F32))
        acc = acc + gate * _mm(o, wb_ref[kbr])
    out_ref[...] = x + _mm(acc, wo_ref[...])


def _merge(x2, nw, w_gate, o_a, o_b, o_c, o_d, w_branch, w_out, layer):
    T, D = x2.shape
    tm = ROW_TILE
    row = lambda w: pl.BlockSpec((tm, w), lambda i: (i, 0))
    return pl.pallas_call(
        _merge_kernel,
        grid=(T // tm,),
        in_specs=[row(D), _const_spec((1, D)), _layer_spec(w_gate.shape, layer)] + [row(MIX_WIDTH)] * N_BRANCH
                 + [_layer_spec(w_branch.shape, layer), _layer_spec(w_out.shape, layer)],
        out_specs=row(D),
        out_shape=jax.ShapeDtypeStruct((T, D), F32),
        compiler_params=_params(1),
        name="merge",
    )(x2, nw, w_gate, o_a, o_b, o_c, o_d, w_branch, w_out)


def _ffn_kernel(x_ref, nw_ref, wup_ref, cw_ref, cb_ref, wdn_ref, fw_ref, out_ref, tail_ref, *, n_s, final_norm):
    tm = x_ref.shape[0]

    @pl.when(pl.program_id(0) % n_s == 0)
    def _():
        tail_ref[...] = jnp.zeros_like(tail_ref)

    x = x_ref[...]
    hb = _rms(x, nw_ref[...]).astype(BF16)
    up = jnp.dot(hb, wup_ref[:, :D_FF], preferred_element_type=F32)
    gate = jnp.dot(hb, wup_ref[:, D_FF:], preferred_element_type=F32)
    row = lax.broadcasted_iota(jnp.int32, up.shape, 0)
    tail = tail_ref[...]
    up1 = jnp.where(row == 0, tail[1:2, :], pltpu.roll(up, 1, 0))
    up2 = jnp.where(row == 0, tail[0:1, :], jnp.where(row == 1, tail[1:2, :], pltpu.roll(up, 2, 0)))
    tail_ref[...] = up[tm - 2:tm, :]
    cw = cw_ref[...]
    conv = cb_ref[...] + cw[0:1, :] * up + cw[1:2, :] * up1 + cw[2:3, :] * up2
    act = conv * _sigmoid(conv) * gate
    y = x + _mm(act, wdn_ref[...])
    out_ref[...] = _rms(y, fw_ref[...]) if final_norm else y


def _ffn(x2, nw, w_up, conv_w, conv_b, w_down, final_w, layer, seq, final_norm):
    T, D = x2.shape
    tm = ROW_TILE
    row = pl.BlockSpec((tm, D), lambda i: (i, 0))
    return pl.pallas_call(
        functools.partial(_ffn_kernel, n_s=seq // tm, final_norm=final_norm),
        grid=(T // tm,),
        in_specs=[row, _const_spec((1, D)), _layer_spec(w_up.shape, layer), _const_spec(conv_w.shape),
                  _const_spec((1, D_FF)), _layer_spec(w_down.shape, layer), _const_spec((1, D))],
        out_specs=row,
        out_shape=jax.ShapeDtypeStruct((T, D), F32),
        scratch_shapes=[pltpu.VMEM((CONV_W - 1, D_FF), F32)],
        compiler_params=_params(1),
        name="convglu",
    )(x2, nw, w_up, conv_w, conv_b, w_down, final_w)


def kernel(x, norm_mix_w, norm_ffn_w, norm_final_w, w_in, hg_lb_table, hg_norm_w, gla_a_w2, gla_a_b, gla_norm_w, rw_mu, rw_w0, rw_w2, rw_a0, rw_a2, rw_g2, rw_k_k, rw_k_a, rw_r_k, rw_ln_w, rw_ln_b, rw_v0, rw_v1, rw_v2, w_branch, w_out, ffn_w_up, ffn_conv_w, ffn_conv_b, ffn_w_down):
    B, S, D = x.shape
    T = B * S
    x2 = x.reshape(T, D)
    cos_t, sin_t = _rope_tables(S)
    row = lambda t: t.reshape(1, -1)
    W = MIX_WIDTH
    src = np.concatenate([np.arange(0, W), np.arange(W + RW_DECAY_LORA, 3 * W + RW_DECAY_LORA),
                          np.arange(W, W + RW_DECAY_LORA), np.arange(3 * W + RW_DECAY_LORA, RW_WIDTH)])
    rw_runs = [(0, W), (W + RW_DECAY_LORA, 3 * W + RW_DECAY_LORA), (W, W + RW_DECAY_LORA),
               (3 * W + RW_DECAY_LORA, RW_WIDTH)]
    at_runs = [(j * AT_WIDTH + g * W, j * AT_WIDTH + (g + 1) * W) for g in range(N_DIL) for j in range(3)]
    o_hg, o_gla, o_rw, o_at = 0, HG_WIDTH, HG_WIDTH + GLA_WIDTH, HG_WIDTH + GLA_WIDTH + RW_WIDTH
    o_gate = o_at + 3 * AT_WIDTH
    runs = ([(o_hg, o_rw)] + [None] + [(o_rw + a, o_rw + b) for a, b in rw_runs]
            + [(o_at + a, o_at + b) for a, b in at_runs])
    gap = jnp.zeros((DEPTH, D, _MIX_RW - _MIX_GLA - GLA_WIDTH), w_in.dtype)
    w_mix = jnp.concatenate([gap if r is None else w_in[:, :, r[0]:r[1]] for r in runs], axis=2).astype(BF16)
    w_gate = w_in[:, :, o_gate:].astype(BF16)
    w_branch_b, w_out_b = w_branch.astype(BF16), w_out.astype(BF16)
    w_up_b, w_down_b = ffn_w_up.astype(BF16), ffn_w_down.astype(BF16)
    v_first = None
    for layer in range(DEPTH):
        z_hg, z_gla, z_rw, *at = _inproj(x2, row(norm_mix_w[layer]), cos_t, sin_t, w_mix, layer, S)

        o_a = _hgrn2(z_hg, hg_lb_table, row(hg_norm_w[layer]), layer, S)
        o_b = _gla(z_gla, gla_a_w2[layer].astype(BF16), row(gla_a_b[layer]), row(gla_norm_w[layer]), S)
        p = {"mu": row(rw_mu[layer][src]), "w0": row(rw_w0[layer]), "w2": rw_w2[layer].astype(BF16),
             "a0": row(rw_a0[layer]), "a2": rw_a2[layer].astype(BF16), "g2": rw_g2[layer].astype(BF16),
             "k_k": row(rw_k_k[layer]), "k_a": row(rw_k_a[layer]), "r_k": row(rw_r_k[layer]),
             "ln_w": row(rw_ln_w[layer]), "ln_b": row(rw_ln_b[layer])}
        if layer > 0:
            p.update(v0=row(rw_v0[layer - 1]), v1=rw_v1[layer - 1].astype(BF16), v2=rw_v2[layer - 1].astype(BF16))
        o_c, v_new = _rwkv(z_rw, v_first, p, S)
        if layer == 0:
            v_first = v_new
        o_d = _attention(at, B, S)

        x2 = _merge(x2, row(norm_mix_w[layer]), w_gate, o_a, o_b, o_c, o_d, w_branch_b, w_out_b, layer)
        x2 = _ffn(x2, row(norm_ffn_w[layer]), w_up_b, ffn_conv_w[layer],
                  row(ffn_conv_b[layer]), w_down_b, row(norm_final_w), layer, S,
                  final_norm=(layer == DEPTH - 1))
    return x2.reshape(B, S, D)
```

```python
import functools

import numpy as np
import jax
import jax.numpy as jnp
from jax import lax
from jax.experimental import pallas as pl
from jax.experimental.pallas import tpu as pltpu

F32 = jnp.float32
BF16 = jnp.bfloat16

D_MODEL = 1024
DEPTH = 2
HEAD_DIM = 64
MIX_WIDTH = D_MODEL // 4
N_BRANCH = 4
NORM_EPS = 1e-6
MASK_VALUE = -1e9
EXP_CLIP = 60.0
N_HEADS = MIX_WIDTH // HEAD_DIM
GLA_DK = 32
GLA_RANK = 16
GLA_TAU = 16.0
RW_DECAY_LORA = 32
RW_AAA_LORA = 32
RW_MV_LORA = 32
RW_GATE_LORA = 64
RW_GN_EPS = 64e-5
DIL_PAIRS = ((128, 1), (512, 4), (2048, 16))
N_DIL = 3
AT_WIDTH = N_DIL * N_HEADS * HEAD_DIM
ROPE_THETA = 10000.0
D_FF = 11 * D_MODEL // 4
CONV_W = 3

HG_WIDTH = 4 * MIX_WIDTH
GLA_QK = N_HEADS * GLA_DK
GLA_WIDTH = 2 * GLA_QK + 2 * MIX_WIDTH + GLA_RANK
RW_WIDTH = 3 * MIX_WIDTH + RW_DECAY_LORA + RW_AAA_LORA + RW_GATE_LORA
GATE_WIDTH = N_BRANCH * D_MODEL

LANES = 128
LOG2_E = 1.4426950408889634
VMEM_LIMIT_BYTES = 56 * 1024 * 1024
ROW_TILE = 512
GLA_TIME_BLOCK = 1024
RW_TIME_BLOCK = 512
GLA_CHUNK = 128
RW_CHUNK = 64
ATT_BLOCK = 128


def _const_spec(shape):
    nd = len(shape)
    return pl.BlockSpec(shape, lambda *_: (0,) * nd, pipeline_mode=pl.Buffered(1))


def _layer_spec(shape, layer):
    nd = len(shape)
    return pl.BlockSpec((None,) + tuple(shape[1:]), lambda *_: (layer,) + (0,) * (nd - 1), pipeline_mode=pl.Buffered(1))


def _params(n_grid, independent=()):
    sem = tuple("parallel" if a in independent else "arbitrary" for a in range(n_grid))
    return pltpu.CompilerParams(dimension_semantics=sem, vmem_limit_bytes=VMEM_LIMIT_BYTES)


def _mm(a, b):
    return jnp.dot(a.astype(BF16), b.astype(BF16), preferred_element_type=F32)


def _mm_nt(a, b):
    return lax.dot_general(a.astype(BF16), b.astype(BF16), (((1,), (1,)), ((), ())), preferred_element_type=F32)


def _mm_tn(a, b):
    return lax.dot_general(a.astype(BF16), b.astype(BF16), (((0,), (0,)), ((), ())), preferred_element_type=F32)


def _split2(a):
    hi = a.astype(BF16)
    lo = (a - hi.astype(F32)).astype(BF16)
    return hi, lo


def _mm_sel_lhs(sel, a):
    hi, lo = _split2(a)
    return jnp.dot(sel, hi, preferred_element_type=F32) + jnp.dot(sel, lo, preferred_element_type=F32)


def _mm_sel_rhs(a, sel):
    hi, lo = _split2(a)
    return jnp.dot(hi, sel, preferred_element_type=F32) + jnp.dot(lo, sel, preferred_element_type=F32)


def _rms(x, w):
    return x * lax.rsqrt(jnp.mean(x * x, axis=-1, keepdims=True) + NORM_EPS) * w


def _sigmoid(x):
    return jax.nn.sigmoid(x)


def _log_sigmoid(x):
    return jnp.minimum(x, 0.0) - jnp.log1p(jnp.exp(-jnp.abs(x)))


def _softplus(x):
    return jnp.maximum(x, 0.0) + jnp.log1p(jnp.exp(-jnp.abs(x)))


def _n_levels(chunk):
    return int(np.log2(chunk))


def _prefix_matrix(chunk):
    C, L = chunk, _n_levels(chunk)
    M = np.zeros(((L + 1) * C, C), np.float32)
    M[:C] = np.tril(np.ones((C, C), np.float32))
    for l in range(L):
        m = C >> (l + 1)
        for t in range(C):
            mid = (t // (2 * m)) * 2 * m + m - 1
            if t % (2 * m) >= m:
                M[(l + 1) * C + t, mid + 1:t + 1] = 1.0
            else:
                M[(l + 1) * C + t, t + 1:mid + 1] = 1.0
    return jnp.asarray(M, BF16)


def _block_ones(n_groups, w_in, w_out):
    return jnp.asarray(np.kron(np.eye(n_groups, dtype=np.float32), np.ones((w_in, w_out), np.float32)), BF16)


def _rope_tables(seq):
    half = HEAD_DIM // 2
    inv = ROPE_THETA ** (-np.arange(half, dtype=np.float64) / half)
    ang = np.arange(seq, dtype=np.float64)[:, None] * inv[None, :]
    cos, sin = np.cos(ang), np.sin(ang)
    cos_h = np.concatenate([cos, cos], axis=1)
    sin_h = np.concatenate([-sin, sin], axis=1)
    return (jnp.asarray(np.tile(cos_h, (1, N_HEADS)), F32), jnp.asarray(np.tile(sin_h, (1, N_HEADS)), F32))


QKV_WIDTH = 3 * MIX_WIDTH


_MIX_HG = 0
_MIX_GLA = _MIX_HG + HG_WIDTH
_MIX_RW = _MIX_GLA + -(-GLA_WIDTH // LANES) * LANES
_MIX_AT = _MIX_RW + RW_WIDTH
MIX_PROJ_WIDTH = _MIX_AT + 3 * AT_WIDTH


def _inproj_kernel(x_ref, nw_ref, cos_ref, sin_ref, w_ref,
                   zhg_ref, zgla_ref, zrw_ref, at1_ref, at4_ref, at16_ref, at_scr):
    tm = x_ref.shape[0]
    hb = _rms(x_ref[...], nw_ref[...]).astype(BF16)
    zat = jnp.dot(hb, w_ref[:, _MIX_AT:], preferred_element_type=F32)
    cos, sin = cos_ref[...], sin_ref[...]
    lane = lax.broadcasted_iota(jnp.int32, cos.shape, 1)
    first_half = (lane % HEAD_DIM) < (HEAD_DIM // 2)

    def rope(p):
        partner = jnp.where(first_half, pltpu.roll(p, MIX_WIDTH - HEAD_DIM // 2, 1), pltpu.roll(p, HEAD_DIM // 2, 1))
        return p * cos + partner * sin

    n_lane_blk = QKV_WIDTH // LANES
    for g, (out_ref, (_, dil)) in enumerate(zip((at1_ref, at4_ref, at16_ref), DIL_PAIRS)):
        c0 = g * QKV_WIDTH
        q = rope(zat[:, c0:c0 + MIX_WIDTH]) * (HEAD_DIM ** -0.5 * LOG2_E)
        k = rope(zat[:, c0 + MIX_WIDTH:c0 + 2 * MIX_WIDTH])
        qkv = jnp.concatenate([q, k, zat[:, c0 + 2 * MIX_WIDTH:c0 + QKV_WIDTH]], axis=1)
        if dil == 1:
            out_ref[...] = qkv.astype(BF16)
            continue
        for c in range(n_lane_blk):
            at_scr[c] = qkv[:, c * LANES:(c + 1) * LANES]
        for r in range(dil):
            rows = jnp.concatenate([at_scr[c, pl.ds(r, tm // dil, stride=dil), :] for c in range(n_lane_blk)], axis=1)
            out_ref[:, r * QKV_WIDTH:(r + 1) * QKV_WIDTH] = rows.astype(BF16)
    zhg_ref[...] = jnp.dot(hb, w_ref[:, _MIX_HG:_MIX_HG + HG_WIDTH], preferred_element_type=F32)
    zgla_ref[...] = jnp.dot(hb, w_ref[:, _MIX_GLA:_MIX_GLA + GLA_WIDTH], preferred_element_type=F32)
    zrw_ref[...] = jnp.dot(hb, w_ref[:, _MIX_RW:_MIX_RW + RW_WIDTH], preferred_element_type=F32)


def _inproj(x2, nw, cos_t, sin_t, w_mix, layer, seq):
    T, D = x2.shape
    tm = ROW_TILE
    n_s = seq // tm
    row = lambda w: pl.BlockSpec((tm, w), lambda i: (i, 0))
    pos = pl.BlockSpec((tm, MIX_WIDTH), lambda i: (i % n_s, 0))
    widths = (HG_WIDTH, GLA_WIDTH, RW_WIDTH)
    dils = [d for _, d in DIL_PAIRS]
    return pl.pallas_call(
        _inproj_kernel,
        grid=(T // tm,),
        in_specs=[row(D), _const_spec((1, D)), pos, pos, _layer_spec(w_mix.shape, layer)],
        out_specs=[row(w) for w in widths]
                  + [pl.BlockSpec((tm // d, d * QKV_WIDTH), lambda i: (i, 0)) for d in dils],
        out_shape=[jax.ShapeDtypeStruct((T, w), F32) for w in widths]
                  + [jax.ShapeDtypeStruct((T // d, d * QKV_WIDTH), BF16) for d in dils],
        scratch_shapes=[pltpu.VMEM((QKV_WIDTH // LANES, tm, LANES), F32)],
        compiler_params=_params(1, independent=(0,)),
        name="inproj",
    )(x2, nw, cos_t, sin_t, w_mix)


def _level_masks(chunk):
    t = lax.broadcasted_iota(jnp.int32, (chunk, chunk), 0)
    s = lax.broadcasted_iota(jnp.int32, (chunk, chunk), 1)
    masks = []
    for l in range(_n_levels(chunk)):
        m = chunk >> (l + 1)
        sh = int(np.log2(2 * m))
        masks.append(((t >> sh) == (s >> sh)) & ((t & m) != 0) & ((s & m) == 0))
    return masks


def _gla_block(q, k, v, g, st_ref, pm_ref, qk_ones, *, chunk, dk, dv):
    C = chunk
    nc = q.shape[0] // C
    n_lvl = _n_levels(C)
    masks = _level_masks(C)
    pm = pm_ref[...]
    rows = lambda t, c: t[c * C:(c + 1) * C]
    stack = lambda f: jnp.concatenate([f(c) for c in range(nc)], axis=0)
    g2 = g * LOG2_E
    D = [_mm_sel_lhs(pm, rows(g2, c)) for c in range(nc)]
    b = stack(lambda c: D[c][:C])
    b_last = stack(lambda c: jnp.broadcast_to(D[c][C - 1:C, :], (C, b.shape[1])))
    dec_last = [jnp.exp2(D[c][C - 1:C, :]) for c in range(nc)]
    q_in = (q * jnp.exp2(b)).astype(BF16)
    k_dec = (k * jnp.exp2(b_last - b)).astype(BF16)
    vb = v.astype(BF16)
    qf, kf = [], []
    for l in range(n_lvl):
        e = jnp.exp2(stack(lambda c: D[c][(l + 1) * C:(l + 2) * C]))
        qf.append((q * e).astype(BF16))
        kf.append((k * e).astype(BF16))
    diag = _mm(q * k, qk_ones)
    pairs = [(c, h) for c in range(nc) for h in range(N_HEADS)]
    bk = lambda t, c, h: t[c * C:(c + 1) * C, h * dk:(h + 1) * dk]
    bv = lambda t, c, h: t[c * C:(c + 1) * C, h * dv:(h + 1) * dv]
    nt = (((1,), (1,)), ((), ()))
    tn = (((0,), (0,)), ((), ()))
    prod = {(p, l): lax.dot_general(bk(qf[l], *p), bk(kf[l], *p), nt, preferred_element_type=F32)
            for p in pairs for l in range(n_lvl)}
    o_intra, kv = {}, {}
    for p in pairs:
        sc = jnp.zeros((C, C), F32)
        for l in range(n_lvl):
            sc = jnp.where(masks[l], prod[p, l], sc)
        o_intra[p] = jnp.dot(sc.astype(BF16), bv(vb, *p), preferred_element_type=F32)
        kv[p] = lax.dot_general(bv(vb, *p), bk(k_dec, *p), tn, preferred_element_type=F32)
    s = [st_ref[h] for h in range(N_HEADS)]
    out_rows = []
    for c in range(nc):
        o_inter = [lax.dot_general(bk(q_in, c, h), s[h].astype(BF16), nt, preferred_element_type=F32)
                   for h in range(N_HEADS)]
        s = [s[h] * dec_last[c][:, h * dk:(h + 1) * dk] + kv[c, h] for h in range(N_HEADS)]
        out_rows.append(jnp.concatenate([o_intra[c, h] + o_inter[h] for h in range(N_HEADS)], axis=1))
    for h in range(N_HEADS):
        st_ref[h] = s[h]
    return jnp.concatenate(out_rows, axis=0) + diag * v


def _head_rmsnorm(o, head_ones, w):
    ms = _mm(o * o, head_ones) * (1.0 / HEAD_DIM)
    return o * lax.rsqrt(ms + NORM_EPS) * w


def _hgrn2_kernel(z_ref, lbt_ref, nw_ref, pm_ref, ones_ref, o_ref, st_ref, *, layer):
    @pl.when(pl.program_id(1) == 0)
    def _():
        st_ref[...] = jnp.zeros_like(st_ref)

    tab = lbt_ref[...]
    e = jnp.exp(tab - jnp.max(tab, axis=0, keepdims=True))
    p = e / jnp.sum(e, axis=0, keepdims=True)
    lb = jnp.zeros((1, MIX_WIDTH), F32)
    for i in range(1, layer + 1):
        lb = lb + p[i:i + 1, :]
    ones = ones_ref[...]
    z = z_ref[...]
    q_raw, fz = z[:, :MIX_WIDTH], z[:, MIX_WIDTH:2 * MIX_WIDTH]
    i_raw, g_raw = z[:, 2 * MIX_WIDTH:3 * MIX_WIDTH], z[:, 3 * MIX_WIDTH:]
    log_f = _log_sigmoid(fz) + jnp.log1p(lb * jnp.exp(jnp.minimum(-fz, EXP_CLIP)))
    k = (1.0 - lb) * _sigmoid(-fz)
    q = q_raw * _sigmoid(q_raw)
    o = _gla_block(q, k, i_raw, log_f, st_ref, pm_ref, ones, chunk=GLA_CHUNK, dk=HEAD_DIM, dv=HEAD_DIM)
    o_ref[...] = (_head_rmsnorm(o, ones, nw_ref[...]) * _sigmoid(g_raw)).astype(o_ref.dtype)


def _gla_kernel(z_ref, aw2_ref, ab_ref, nw_ref, pm_ref, qk_ones_ref, ones_ref, o_ref, st_ref):
    @pl.when(pl.program_id(1) == 0)
    def _():
        st_ref[...] = jnp.zeros_like(st_ref)

    z = z_ref[...]
    q = z[:, :GLA_QK] * (GLA_DK ** -0.5)
    k = z[:, GLA_QK:2 * GLA_QK]
    v = z[:, 2 * GLA_QK:2 * GLA_QK + MIX_WIDTH]
    gate = z[:, 2 * GLA_QK + MIX_WIDTH:2 * GLA_QK + 2 * MIX_WIDTH]
    a_low = z[:, 2 * GLA_QK + 2 * MIX_WIDTH:]
    log_alpha = _log_sigmoid(_mm(a_low, aw2_ref[...]) + ab_ref[...]) * (1.0 / GLA_TAU)
    o = _gla_block(q, k, v, log_alpha, st_ref, pm_ref, qk_ones_ref[...], chunk=GLA_CHUNK, dk=GLA_DK, dv=HEAD_DIM)
    o_ref[...] = (_head_rmsnorm(o, ones_ref[...], nw_ref[...]) * (gate * _sigmoid(gate))).astype(o_ref.dtype)


def _time_grid(T, seq, ts):
    n_t = seq // ts
    return ts, n_t, (T // seq, n_t)


def _hgrn2(z_hg, lb_table, norm_w, layer, seq):
    T = z_hg.shape[0]
    ts, n_t, grid = _time_grid(T, seq, GLA_TIME_BLOCK)
    blk = lambda w: pl.BlockSpec((ts, w), lambda b, i: (b * n_t + i, 0))
    pm = _prefix_matrix(GLA_CHUNK)
    ones = _block_ones(N_HEADS, HEAD_DIM, HEAD_DIM)
    return pl.pallas_call(
        functools.partial(_hgrn2_kernel, layer=layer),
        grid=grid,
        in_specs=[blk(HG_WIDTH), _const_spec(lb_table.shape), _const_spec((1, MIX_WIDTH)),
                  _const_spec(pm.shape), _const_spec(ones.shape)],
        out_specs=blk(MIX_WIDTH),
        out_shape=jax.ShapeDtypeStruct((T, MIX_WIDTH), BF16),
        scratch_shapes=[pltpu.VMEM((N_HEADS, HEAD_DIM, HEAD_DIM), F32)],
        compiler_params=_params(2, independent=(0,)),
        name="hgrn2",
    )(z_hg, lb_table, norm_w, pm, ones)


def _gla(z_gla, a_w2, a_b, norm_w, seq):
    T = z_gla.shape[0]
    ts, n_t, grid = _time_grid(T, seq, GLA_TIME_BLOCK)
    blk = lambda w: pl.BlockSpec((ts, w), lambda b, i: (b * n_t + i, 0))
    pm = _prefix_matrix(GLA_CHUNK)
    qk_ones = _block_ones(N_HEADS, GLA_DK, HEAD_DIM)
    ones = _block_ones(N_HEADS, HEAD_DIM, HEAD_DIM)
    return pl.pallas_call(
        _gla_kernel,
        grid=grid,
        in_specs=[blk(GLA_WIDTH), _const_spec(a_w2.shape), _const_spec((1, GLA_QK)), _const_spec((1, MIX_WIDTH)),
                  _const_spec(pm.shape), _const_spec(qk_ones.shape), _const_spec(ones.shape)],
        out_specs=blk(MIX_WIDTH),
        out_shape=jax.ShapeDtypeStruct((T, MIX_WIDTH), BF16),
        scratch_shapes=[pltpu.VMEM((N_HEADS, HEAD_DIM, GLA_DK), F32)],
        compiler_params=_params(2, independent=(0,)),
        name="gla",
    )(z_gla, a_w2, a_b, norm_w, pm, qk_ones, ones)


_RW_R, _RW_K, _RW_V = 0, MIX_WIDTH, 2 * MIX_WIDTH
_RW_WL = 3 * MIX_WIDTH
_RW_AL = _RW_WL + RW_DECAY_LORA
_RW_GL = _RW_AL + RW_AAA_LORA


def _rwkv_kernel(*refs, has_vfirst):
    if has_vfirst:
        (z_ref, vf_ref, mu_ref, w0_ref, w2_ref, a0_ref, a2_ref, g2_ref, kk_ref, ka_ref, rk_ref, lnw_ref, lnb_ref,
         v0_ref, v1_ref, v2_ref, tri_ref, ones_ref, o_ref, s_ref, carry_ref) = refs
    else:
        (z_ref, mu_ref, w0_ref, w2_ref, a0_ref, a2_ref, g2_ref, kk_ref, ka_ref, rk_ref, lnw_ref, lnb_ref,
         tri_ref, ones_ref, o_ref, vout_ref, s_ref, carry_ref) = refs
    C, HD = RW_CHUNK, HEAD_DIM
    ts = z_ref.shape[0]
    nc = ts // C
    pairs = [(c, h) for c in range(nc) for h in range(N_HEADS)]

    @pl.when(pl.program_id(1) == 0)
    def _():
        s_ref[...] = jnp.zeros_like(s_ref)
        carry_ref[...] = jnp.zeros_like(carry_ref)

    z = z_ref[...]
    zp = pltpu.roll(z, 1, 0)
    zp = jnp.where(lax.broadcasted_iota(jnp.int32, z.shape, 0) == 0, carry_ref[...], zp)
    carry_ref[...] = z[ts - 1:ts, :]
    zs = z + mu_ref[...] * (zp - z)
    r, k, v = zs[:, _RW_R:_RW_K], zs[:, _RW_K:_RW_V], zs[:, _RW_V:_RW_WL]
    wl, al, gl = zs[:, _RW_WL:_RW_AL], zs[:, _RW_AL:_RW_GL], zs[:, _RW_GL:]
    ones = ones_ref[...]
    gsum = lambda t: _mm_sel_rhs(t, ones)
    w = -_softplus(-(w0_ref[...] + _mm(jnp.tanh(wl), w2_ref[...]))) - 0.5
    lw = -jnp.exp(w)
    a = _sigmoid(a0_ref[...] + _mm(al, a2_ref[...]))
    g = _mm(_sigmoid(gl), g2_ref[...])
    if has_vfirst:
        v = v + (vf_ref[...] - v) * _sigmoid(v0_ref[...] + _mm(_mm(v, v1_ref[...]), v2_ref[...]))
    else:
        vout_ref[...] = v
    kk = k * kk_ref[...]
    kk = kk / jnp.maximum(jnp.sqrt(gsum(kk * kk)), 1e-12)
    k_mod = k * (1.0 + (a - 1.0) * ka_ref[...])
    kka = kk * a

    tri = tri_ref[...]
    cums = [_mm_sel_lhs(tri, lw[c * C:(c + 1) * C]) for c in range(nc)]
    cum = jnp.concatenate(cums, axis=0)
    cum_last = jnp.concatenate([jnp.broadcast_to(cc[C - 1:C, :], (C, MIX_WIDTH)) for cc in cums], axis=0)
    w_end = [jnp.exp(cc[C - 1:C, :]) for cc in cums]
    r_t = r * jnp.exp(cum)
    a_t = -kk * jnp.exp(cum - lw)
    inv = jnp.exp(-cum)
    end = jnp.exp(cum_last - cum)
    r_tb, a_tb = r_t.astype(BF16), a_t.astype(BF16)
    k_hb, b_hb = (k_mod * inv).astype(BF16), (kka * inv).astype(BF16)
    k_eb, b_eb = (k_mod * end).astype(BF16), (kka * end).astype(BF16)
    vb = v.astype(BF16)
    blk = lambda t, c, h: t[c * C:(c + 1) * C, h * HD:(h + 1) * HD]

    ti = lax.broadcasted_iota(jnp.int32, (C, C), 0)
    si = lax.broadcasted_iota(jnp.int32, (C, C), 1)
    incl, strict = si <= ti, si < ti
    dot = lambda x, y: jnp.dot(x, y, preferred_element_type=F32)
    aa = {p: _mm_nt(jnp.concatenate([blk(r_tb, *p), blk(a_tb, *p)], axis=0),
                    jnp.concatenate([blk(k_hb, *p), blk(b_hb, *p)], axis=0)) for p in pairs}
    incl2 = (lax.broadcasted_iota(jnp.int32, (C, 2 * C), 1) & (C - 1)) <= lax.broadcasted_iota(jnp.int32, (C, 2 * C), 0)
    a_r = {p: jnp.where(incl2, aa[p][:C], 0.0).astype(BF16) for p in pairs}
    a_ak = {p: jnp.where(strict, aa[p][C:, :C], 0.0).astype(BF16) for p in pairs}
    npow = {p: jnp.where(strict, aa[p][C:, C:], 0.0) for p in pairs}
    pv = {p: dot(a_ak[p], blk(vb, *p)) for p in pairs}
    x = {p: jnp.concatenate([blk(a_t, *p), pv[p]], axis=1) for p in pairs}
    n_stage = _n_levels(C)
    for j in range(n_stage):
        last = j == n_stage - 1
        nb = {p: npow[p].astype(BF16) for p in pairs}
        nx = {p: dot(nb[p], x[p].astype(BF16)) for p in pairs}
        if not last:
            npow = {p: dot(nb[p], nb[p]) for p in pairs}
        x = {p: x[p] + nx[p] for p in pairs}
    rmat = {p: jnp.concatenate([jnp.concatenate([jnp.zeros((C, HD), BF16), blk(vb, *p)], axis=1), x[p].astype(BF16)],
                               axis=0) for p in pairs}
    qz = {p: dot(a_r[p], rmat[p]) for p in pairs}
    gh = {p: lax.dot_general(rmat[p], jnp.concatenate([blk(k_eb, *p), blk(b_eb, *p)], axis=0),
                             (((0,), (0,)), ((), ())), preferred_element_type=F32) for p in pairs}
    q_f = {p: (blk(r_t, *p) + qz[p][:, :HD]).astype(BF16) for p in pairs}

    s = [s_ref[h] for h in range(N_HEADS)]
    y_rows = []
    for c in range(nc):
        sb = [s[h].astype(BF16) for h in range(N_HEADS)]
        ys = [lax.dot_general(q_f[c, h], sb[h], (((1,), (1,)), ((), ())), preferred_element_type=F32)
              + qz[c, h][:, HD:] for h in range(N_HEADS)]
        sg = [dot(sb[h], gh[c, h][:HD].astype(BF16)) for h in range(N_HEADS)]
        s = [s[h] * w_end[c][:, h * HD:(h + 1) * HD] + sg[h] + gh[c, h][HD:] for h in range(N_HEADS)]
        y_rows.append(jnp.concatenate(ys, axis=1))
    for h in range(N_HEADS):
        s_ref[h] = s[h]
    y = jnp.concatenate(y_rows, axis=0)

    hsum = lambda t: _mm(t, ones)
    mean = hsum(y) * (1.0 / HD)
    yc = y - mean
    var = hsum(yc * yc) * (1.0 / HD)
    yn = yc * lax.rsqrt(var + RW_GN_EPS) * lnw_ref[...] + lnb_ref[...]
    bonus = hsum(r * k_mod * rk_ref[...]) * v
    o_ref[...] = ((yn + bonus) * g).astype(o_ref.dtype)


def _rwkv(z_rw, v_first, p, seq):
    T = z_rw.shape[0]
    ts, n_t, grid = _time_grid(T, seq, RW_TIME_BLOCK)
    blk = lambda w: pl.BlockSpec((ts, w), lambda b, i: (b * n_t + i, 0))
    tri = jnp.asarray(np.tril(np.ones((RW_CHUNK, RW_CHUNK), np.float32)), BF16)
    ones = _block_ones(N_HEADS, HEAD_DIM, HEAD_DIM)
    has_vfirst = v_first is not None
    names = ["mu", "w0", "w2", "a0", "a2", "g2", "k_k", "k_a", "r_k", "ln_w", "ln_b"]
    args = [z_rw] + ([v_first] if has_vfirst else []) + [p[n] for n in names]
    specs = [blk(RW_WIDTH)] + ([blk(MIX_WIDTH)] if has_vfirst else []) + [_const_spec(p[n].shape) for n in names]
    if has_vfirst:
        args += [p["v0"], p["v1"], p["v2"]]
        specs += [_const_spec(p[n].shape) for n in ("v0", "v1", "v2")]
    args += [tri, ones]
    specs += [_const_spec(tri.shape), _const_spec(ones.shape)]
    out_sds = jax.ShapeDtypeStruct((T, MIX_WIDTH), BF16)
    v_sds = jax.ShapeDtypeStruct((T, MIX_WIDTH), F32)
    res = pl.pallas_call(
        functools.partial(_rwkv_kernel, has_vfirst=has_vfirst),
        grid=grid,
        in_specs=specs,
        out_specs=blk(MIX_WIDTH) if has_vfirst else [blk(MIX_WIDTH), blk(MIX_WIDTH)],
        out_shape=out_sds if has_vfirst else [out_sds, v_sds],
        scratch_shapes=[pltpu.VMEM((N_HEADS, HEAD_DIM, HEAD_DIM), F32), pltpu.VMEM((1, RW_WIDTH), F32)],
        compiler_params=_params(2, independent=(0,)),
        name="rwkv7",
    )(*args)
    return (res, None) if has_vfirst else (res[0], res[1])


ATT_TOKENS = ATT_BLOCK * max(d for _, d in DIL_PAIRS)
ATT_UNITS_PER_BATCH = 2


def _attn_kernel(c1_ref, p1_ref, c4_ref, p4_ref, c16_ref, p16_ref, o_ref, acc_scr, m_scr, l_scr):
    qb = ATT_BLOCK
    n_tile = MIX_WIDTH // LANES
    row = lax.broadcasted_iota(jnp.int32, (qb, 2 * qb), 0)
    col = lax.broadcasted_iota(jnp.int32, (qb, 2 * qb), 1)
    band = (col >= row) & (col <= row + qb)
    band_first = band & (col >= jnp.where(pl.program_id(1) > 0, 0, qb))
    low = lax.broadcasted_iota(jnp.int32, (qb, LANES), 1) < HEAD_DIM
    nt = (((1,), (1,)), ((), ()))
    curs, prevs = (c1_ref, c4_ref, c16_ref), (p1_ref, p4_ref, p16_ref)
    units = [(g, dil, r, qi) for g, (_, dil) in enumerate(DIL_PAIRS) for r in range(dil)
             for qi in range(ATT_TOKENS // dil // qb)]
    pairs = [(t, e) for t in range(n_tile) for e in range(LANES // HEAD_DIM)]
    tile = lambda x, t: x[:, t * LANES:(t + 1) * LANES]
    for b0 in range(0, len(units), ATT_UNITS_PER_BATCH):
        batch = units[b0:b0 + ATT_UNITS_PER_BATCH]
        vs, masks, scores = [], [], []
        for g, dil, r, qi in batch:
            cur, prev = curs[g], prevs[g]
            qc, kc, vc = (r * QKV_WIDTH + j * MIX_WIDTH for j in range(3))
            q = cur[0, qi * qb:(qi + 1) * qb, qc:qc + MIX_WIDTH]
            if qi == 0:
                k = jnp.concatenate([prev[0, :, kc:kc + MIX_WIDTH], cur[0, :qb, kc:kc + MIX_WIDTH]], axis=0)
                v = jnp.concatenate([prev[0, :, vc:vc + MIX_WIDTH], cur[0, :qb, vc:vc + MIX_WIDTH]], axis=0)
            else:
                k = cur[0, (qi - 1) * qb:(qi + 1) * qb, kc:kc + MIX_WIDTH]
                v = cur[0, (qi - 1) * qb:(qi + 1) * qb, vc:vc + MIX_WIDTH]
            vs.append(v)
            masks.append(band_first if qi == 0 else band)
            zero = jnp.zeros((qb, LANES), q.dtype)
            scores.append({(t, e): lax.dot_general(jnp.where(low if e == 0 else ~low, tile(q, t), zero), tile(k, t), nt,
                                                   preferred_element_type=F32) for t, e in pairs})
        probs, stats = [], []
        for u in range(len(batch)):
            pu, su = {}, {}
            for p in pairs:
                s = jnp.where(masks[u], scores[u][p], MASK_VALUE)
                m = jnp.max(s, axis=-1, keepdims=True)
                ex = jnp.exp2(s - m)
                su[p] = (m, jnp.sum(ex, axis=-1, keepdims=True))
                pu[p] = ex.astype(BF16)
            probs.append(pu)
            stats.append(su)
        pvs = [{(t, e): jnp.dot(probs[u][t, e], tile(vs[u], t), preferred_element_type=F32) for t, e in pairs}
               for u in range(len(batch))]
        for u, (g, dil, r, qi) in enumerate(batch):
            tok = pl.ds(qi * qb * dil + r, qb, stride=dil) if dil > 1 else pl.ds(qi * qb, qb)
            for t in range(n_tile):
                acc = jnp.where(low, pvs[u][t, 0], pvs[u][t, 1])
                m_new = jnp.where(low, stats[u][t, 0][0], stats[u][t, 1][0])
                l_new = jnp.where(low, stats[u][t, 0][1], stats[u][t, 1][1])
                acc_scr[g, t, tok, :] = acc
                m_scr[g, t, tok, :] = m_new
                l_scr[g, t, tok, :] = l_new
    for t in range(n_tile):
        ms = [m_scr[g, t] for g in range(N_DIL)]
        m_tot = functools.reduce(jnp.maximum, ms)
        es = [jnp.exp2(m - m_tot) for m in ms]
        num = sum(e * acc_scr[g, t] for g, e in enumerate(es))
        o_ref[:, t * LANES:(t + 1) * LANES] = (num / sum(e * l_scr[g, t] for g, e in enumerate(es))).astype(o_ref.dtype)


def _attention(at, batch, seq):
    T = batch * seq
    nblk = seq // ATT_TOKENS
    args, specs = [], []
    for a, (_, d) in zip(at, DIL_PAIRS):
        rows = ATT_TOKENS // d
        n_prev = rows // ATT_BLOCK
        a3 = a.reshape(batch, seq // d, d * QKV_WIDTH)
        args += [a3, a3]
        specs += [pl.BlockSpec((1, rows, d * QKV_WIDTH), lambda b, i: (b, i, 0)),
                  pl.BlockSpec((1, ATT_BLOCK, d * QKV_WIDTH), lambda b, i, n=n_prev: (b, jnp.maximum(i * n - 1, 0), 0))]
    return pl.pallas_call(
        _attn_kernel,
        grid=(batch, nblk),
        in_specs=specs,
        out_specs=pl.BlockSpec((ATT_TOKENS, MIX_WIDTH), lambda b, i: (b * nblk + i, 0)),
        out_shape=jax.ShapeDtypeStruct((T, MIX_WIDTH), BF16),
        scratch_shapes=[pltpu.VMEM((N_DIL, MIX_WIDTH // LANES, ATT_TOKENS, LANES), F32)] * 3,
        compiler_params=_params(2, independent=(0, 1)),
        name="dilated_attn",
    )(*args)


def _merge_kernel(x_ref, nw_ref, wg_ref, oa_ref, ob_ref, oc_ref, od_ref, wb_ref, wo_ref, out_ref):
    x = x_ref[...]
    hb = _rms(x, nw_ref[...]).astype(BF16)
    acc = jnp.zeros(x.shape, F32)
    for kbr, o in enumerate((oa_ref[...], ob_ref[...], oc_ref[...], od_ref[...])):
        gate = _sigmoid(jnp.dot(hb, wg_ref[:, kbr * D_MODEL:(kbr + 1) * D_MODEL], preferred_element_type=F32))
        acc = acc + gate * _mm(o, wb_ref[kbr])
    out_ref[...] = x + _mm(acc, wo_ref[...])


def _merge(x2, nw, w_gate, o_a, o_b, o_c, o_d, w_branch, w_out, layer):
    T, D = x2.shape
    tm = ROW_TILE
    row = lambda w: pl.BlockSpec((tm, w), lambda i: (i, 0))
    return pl.pallas_call(
        _merge_kernel,
        grid=(T // tm,),
        in_specs=[row(D), _const_spec((1, D)), _layer_spec(w_gate.shape, layer)] + [row(MIX_WIDTH)] * N_BRANCH
                 + [_layer_spec(w_branch.shape, layer), _layer_spec(w_out.shape, layer)],
        out_specs=row(D),
        out_shape=jax.ShapeDtypeStruct((T, D), F32),
        compiler_params=_params(1, independent=(0,)),
        name="merge",
    )(x2, nw, w_gate, o_a, o_b, o_c, o_d, w_branch, w_out)


def _ffn_kernel(x_ref, nw_ref, wup_ref, cw_ref, cb_ref, wdn_ref, fw_ref, out_ref, tail_ref, *, n_s, final_norm):
    tm = x_ref.shape[0]

    @pl.when(pl.program_id(0) % n_s == 0)
    def _():
        tail_ref[...] = jnp.zeros_like(tail_ref)

    x = x_ref[...]
    hb = _rms(x, nw_ref[...]).astype(BF16)
    up = jnp.dot(hb, wup_ref[:, :D_FF], preferred_element_type=F32)
    gate = jnp.dot(hb, wup_ref[:, D_FF:], preferred_element_type=F32)
    row = lax.broadcasted_iota(jnp.int32, up.shape, 0)
    tail = tail_ref[...]
    up1 = jnp.where(row == 0, tail[1:2, :], pltpu.roll(up, 1, 0))
    up2 = jnp.where(row == 0, tail[0:1, :], jnp.where(row == 1, tail[1:2, :], pltpu.roll(up, 2, 0)))
    tail_ref[...] = up[tm - 2:tm, :]
    cw = cw_ref[...]
    conv = cb_ref[...] + cw[0:1, :] * up + cw[1:2, :] * up1 + cw[2:3, :] * up2
    act = conv * _sigmoid(conv) * gate
    y = x + _mm(act, wdn_ref[...])
    out_ref[...] = _rms(y, fw_ref[...]) if final_norm else y


def _ffn(x2, nw, w_up, conv_w, conv_b, w_down, final_w, layer, seq, final_norm):
    T, D = x2.shape
    tm = ROW_TILE
    row = pl.BlockSpec((tm, D), lambda i: (i, 0))
    return pl.pallas_call(
        functools.partial(_ffn_kernel, n_s=seq // tm, final_norm=final_norm),
        grid=(T // tm,),
        in_specs=[row, _const_spec((1, D)), _layer_spec(w_up.shape, layer), _const_spec(conv_w.shape),
                  _const_spec((1, D_FF)), _layer_spec(w_down.shape, layer), _const_spec((1, D))],
        out_specs=row,
        out_shape=jax.ShapeDtypeStruct((T, D), F32),
        scratch_shapes=[pltpu.VMEM((CONV_W - 1, D_FF), F32)],
        compiler_params=_params(1),
        name="convglu",
    )(x2, nw, w_up, conv_w, conv_b, w_down, final_w)


def kernel(x, norm_mix_w, norm_ffn_w, norm_final_w, w_in, hg_lb_table, hg_norm_w, gla_a_w2, gla_a_b, gla_norm_w, rw_mu, rw_w0, rw_w2, rw_a0, rw_a2, rw_g2, rw_k_k, rw_k_a, rw_r_k, rw_ln_w, rw_ln_b, rw_v0, rw_v1, rw_v2, w_branch, w_out, ffn_w_up, ffn_conv_w, ffn_conv_b, ffn_w_down):
    B, S, D = x.shape
    T = B * S
    x2 = x.reshape(T, D)
    cos_t, sin_t = _rope_tables(S)
    row = lambda t: t.reshape(1, -1)
    W = MIX_WIDTH
    src = np.concatenate([np.arange(0, W), np.arange(W + RW_DECAY_LORA, 3 * W + RW_DECAY_LORA),
                          np.arange(W, W + RW_DECAY_LORA), np.arange(3 * W + RW_DECAY_LORA, RW_WIDTH)])
    rw_runs = [(0, W), (W + RW_DECAY_LORA, 3 * W + RW_DECAY_LORA), (W, W + RW_DECAY_LORA),
               (3 * W + RW_DECAY_LORA, RW_WIDTH)]
    at_runs = [(j * AT_WIDTH + g * W, j * AT_WIDTH + (g + 1) * W) for g in range(N_DIL) for j in range(3)]
    o_hg, o_gla, o_rw, o_at = 0, HG_WIDTH, HG_WIDTH + GLA_WIDTH, HG_WIDTH + GLA_WIDTH + RW_WIDTH
    o_gate = o_at + 3 * AT_WIDTH
    runs = ([(o_hg, o_rw)] + [None] + [(o_rw + a, o_rw + b) for a, b in rw_runs]
            + [(o_at + a, o_at + b) for a, b in at_runs])
    gap = jnp.zeros((DEPTH, D, _MIX_RW - _MIX_GLA - GLA_WIDTH), w_in.dtype)
    w_mix = jnp.concatenate([gap if r is None else w_in[:, :, r[0]:r[1]] for r in runs], axis=2).astype(BF16)
    w_gate = w_in[:, :, o_gate:].astype(BF16)
    w_branch_b, w_out_b = w_branch.astype(BF16), w_out.astype(BF16)
    w_up_b, w_down_b = ffn_w_up.astype(BF16), ffn_w_down.astype(BF16)
    v_first = None
    for layer in range(DEPTH):
        z_hg, z_gla, z_rw, *at = _inproj(x2, row(norm_mix_w[layer]), cos_t, sin_t, w_mix, layer, S)

        o_a = _hgrn2(z_hg, hg_lb_table, row(hg_norm_w[layer]), layer, S)
        o_b = _gla(z_gla, gla_a_w2[layer].astype(BF16), row(gla_a_b[layer]), row(gla_norm_w[layer]), S)
        p = {"mu": row(rw_mu[layer][src]), "w0": row(rw_w0[layer]), "w2": rw_w2[layer].astype(BF16),
             "a0": row(rw_a0[layer]), "a2": rw_a2[layer].astype(BF16), "g2": rw_g2[layer].astype(BF16),
             "k_k": row(rw_k_k[layer]), "k_a": row(rw_k_a[layer]), "r_k": row(rw_r_k[layer]),
             "ln_w": row(rw_ln_w[layer]), "ln_b": row(rw_ln_b[layer])}
        if layer > 0:
            p.update(v0=row(rw_v0[layer - 1]), v1=rw_v1[layer - 1].astype(BF16), v2=rw_v2[layer - 1].astype(BF16))
        o_c, v_new = _rwkv(z_rw, v_first, p, S)
        if layer == 0:
            v_first = v_new
        o_d = _attention(at, B, S)

        x2 = _merge(x2, row(norm_mix_w[layer]), w_gate, o_a, o_b, o_c, o_d, w_branch_b, w_out_b, layer)
        x2 = _ffn(x2, row(norm_ffn_w[layer]), w_up_b, ffn_conv_w[layer],
                  row(ffn_conv_b[layer]), w_down_b, row(norm_final_w), layer, S,
                  final_norm=(layer == DEPTH - 1))
    return x2.reshape(B, S, D)
```
